```python
import jax, jax.numpy as jnp
from jax import lax
import numpy as np

D_MODEL = 1024
BATCH = 16
SEQ = 2048
DEPTH = 1

D_MIX = D_MODEL
D_LRU = D_MIX // 2
D_ATTN = D_MIX - D_LRU
LRU_BLOCKS = 8
LRU_BLOCK_W = D_LRU // LRU_BLOCKS
CONV_W = 4
LRU_C = 8.0
N_HEADS = 8
HEAD_DIM = D_ATTN // N_HEADS
MOBA_BLOCK = 256
MOBA_TOPK = 3
Q_CHUNK = 16
EPS = 1e-6
D_IN = 2 * D_LRU + 4 * D_ATTN

kernel_name = "hymba_rglru_moba_adaln_layer"


def rms_norm(x, g):
    xf = x.astype(jnp.float32)
    y = xf * lax.rsqrt(jnp.mean(xf * xf, axis=-1, keepdims=True) + EPS)
    return (y * g.astype(jnp.float32)).astype(x.dtype)


def causal_depthwise_conv(x, w, b):
    y = lax.conv_general_dilated(
        x, w[:, None, :].astype(x.dtype), window_strides=(1,),
        padding=[(CONV_W - 1, 0)], dimension_numbers=("NWC", "WIO", "NWC"),
        feature_group_count=x.shape[-1])
    return y + b


def rg_lru(xb, w_a, b_a, w_i, b_i, lam):
    B, S, _ = xb.shape
    xg = xb.reshape(B, S, LRU_BLOCKS, LRU_BLOCK_W)
    r = jax.nn.sigmoid(jnp.einsum("bsgi,gij->bsgj", xg, w_a).reshape(B, S, D_LRU) + b_a)
    i = jax.nn.sigmoid(jnp.einsum("bsgi,gij->bsgj", xg, w_i).reshape(B, S, D_LRU) + b_i)
    log_a = -LRU_C * r.astype(jnp.float32) * jax.nn.softplus(-lam.astype(jnp.float32))
    a = jnp.exp(log_a)
    mult = jnp.sqrt(-jnp.expm1(2.0 * log_a))
    first = (jnp.arange(S) == 0)[None, :, None]
    mult = jnp.where(first, 1.0, mult)
    bterm = mult * (i * xb).astype(jnp.float32)

    def combine(lhs, rhs):
        a1, b1 = lhs
        a2, b2 = rhs
        return a1 * a2, a2 * b1 + b2

    _, h = lax.associative_scan(combine, (a, bterm), axis=1)
    return h.astype(xb.dtype)


def moba_attention(q, k, v):
    B, S, H, Dh = q.shape
    s_pad = ((S + MOBA_BLOCK - 1) // MOBA_BLOCK) * MOBA_BLOCK
    pad = ((0, 0), (0, s_pad - S), (0, 0), (0, 0))
    q = jnp.pad(q, pad).transpose(0, 2, 1, 3)
    k = jnp.pad(k, pad).transpose(0, 2, 1, 3)
    v = jnp.pad(v, pad).transpose(0, 2, 1, 3)
    nb = s_pad // MOBA_BLOCK
    kb = k.reshape(B, H, nb, MOBA_BLOCK, Dh)
    vb = v.reshape(B, H, nb, MOBA_BLOCK, Dh)

    k_mean = jnp.mean(kb.astype(jnp.float32), axis=3)
    gate = jnp.einsum("bhsd,bhnd->bhsn", q.astype(jnp.float32), k_mean)
    q_blk = jnp.arange(s_pad) // MOBA_BLOCK
    past = jnp.arange(nb)[None, :] < q_blk[:, None]
    gate = jnp.where(past, gate, -jnp.inf)
    k_sel = max(1, min(MOBA_TOPK, nb - 1))
    top_val, top_idx = lax.top_k(gate, k_sel)
    valid = jnp.isfinite(top_val)

    n_chunks = s_pad // Q_CHUNK
    q_c = jnp.moveaxis(q.reshape(B, H, n_chunks, Q_CHUNK, Dh), 2, 0)
    idx_c = jnp.moveaxis(top_idx.reshape(B, H, n_chunks, Q_CHUNK, k_sel), 2, 0)
    ok_c = jnp.moveaxis(valid.reshape(B, H, n_chunks, Q_CHUNK, k_sel), 2, 0)
    bi = jnp.arange(B)[:, None, None, None]
    hi = jnp.arange(H)[None, :, None, None]
    scale = HEAD_DIM ** -0.5

    def chunk_fn(args):
        ci, qc, idx, ok = args
        k_past = kb[bi, hi, idx]
        v_past = vb[bi, hi, idx]
        own = (ci * Q_CHUNK) // MOBA_BLOCK
        k_own = lax.dynamic_index_in_dim(kb, own, axis=2, keepdims=False)
        v_own = lax.dynamic_index_in_dim(vb, own, axis=2, keepdims=False)
        s_past = jnp.einsum("bhqd,bhqnkd->bhqnk", qc, k_past).astype(jnp.float32) * scale
        s_past = jnp.where(ok[..., None], s_past, -jnp.inf).reshape(B, H, Q_CHUNK, k_sel * MOBA_BLOCK)
        s_own = jnp.einsum("bhqd,bhkd->bhqk", qc, k_own).astype(jnp.float32) * scale
        q_pos = ci * Q_CHUNK + jnp.arange(Q_CHUNK)
        k_pos = own * MOBA_BLOCK + jnp.arange(MOBA_BLOCK)
        s_own = jnp.where(k_pos[None, :] <= q_pos[:, None], s_own, -jnp.inf)
        p = jax.nn.softmax(jnp.concatenate([s_past, s_own], axis=-1), axis=-1).astype(v.dtype)
        p_past = p[..., : k_sel * MOBA_BLOCK].reshape(B, H, Q_CHUNK, k_sel, MOBA_BLOCK)
        p_own = p[..., k_sel * MOBA_BLOCK:]
        return (jnp.einsum("bhqnk,bhqnkd->bhqd", p_past, v_past)
                + jnp.einsum("bhqk,bhkd->bhqd", p_own, v_own))

    out = lax.map(chunk_fn, (jnp.arange(n_chunks), q_c, idx_c, ok_c))
    out = jnp.moveaxis(out, 0, 2).reshape(B, H, s_pad, Dh).transpose(0, 2, 1, 3)
    return out[:, :S]


def hybrid_layer(x, c, w_ada, b_ada, norm_g, w_in, conv_w, conv_b,
                 lru_wa, lru_ba, lru_wi, lru_bi, lru_lambda, q_norm_g, k_norm_g, w_out):
    B, S, _ = x.shape
    ada = jax.nn.silu(c) @ w_ada + b_ada
    shift, scale, gate = jnp.split(ada, 3, axis=-1)
    h = rms_norm(x, norm_g) * (1.0 + scale[:, None, :]) + shift[:, None, :]
    z = h @ w_in
    cuts = [D_LRU, 2 * D_LRU, 2 * D_LRU + D_ATTN, 2 * D_LRU + 2 * D_ATTN, 2 * D_LRU + 3 * D_ATTN]
    xl, gl, q, k, v, ga = jnp.split(z, cuts, axis=-1)

    xl = causal_depthwise_conv(xl, conv_w, conv_b)
    y_lru = rg_lru(xl, lru_wa, lru_ba, lru_wi, lru_bi, lru_lambda) * jax.nn.silu(gl)

    q = rms_norm(q.reshape(B, S, N_HEADS, HEAD_DIM), q_norm_g)
    k = rms_norm(k.reshape(B, S, N_HEADS, HEAD_DIM), k_norm_g)
    v = v.reshape(B, S, N_HEADS, HEAD_DIM)
    y_att = moba_attention(q, k, v).reshape(B, S, D_ATTN) * jax.nn.silu(ga)

    y = jnp.concatenate([y_lru, y_att], axis=-1) @ w_out
    return x + gate[:, None, :] * y


def setup_inputs(seed: int = 0) -> dict:
    key = jax.random.key(seed)
    ks = jax.random.split(key, 16)
    f32 = jnp.float32
    x = jax.random.normal(ks[0], (BATCH, SEQ, D_MODEL), f32)
    c = jax.random.normal(ks[1], (BATCH, D_MODEL), f32)
    w_ada = jax.random.normal(ks[2], (DEPTH, D_MODEL, 3 * D_MODEL), f32) * D_MODEL ** -0.5
    b_ada = jax.random.normal(ks[3], (DEPTH, 3 * D_MODEL), f32) * 0.01
    norm_g = 1.0 + 0.02 * jax.random.normal(ks[4], (DEPTH, D_MODEL), f32)
    w_in = jax.random.normal(ks[5], (DEPTH, D_MODEL, D_IN), f32) * D_MODEL ** -0.5
    conv_w = jax.random.normal(ks[6], (DEPTH, CONV_W, D_LRU), f32) * CONV_W ** -0.5
    conv_b = jax.random.normal(ks[7], (DEPTH, D_LRU), f32) * 0.01
    lru_wa = jax.random.normal(ks[8], (DEPTH, LRU_BLOCKS, LRU_BLOCK_W, LRU_BLOCK_W), f32) * LRU_BLOCK_W ** -0.5
    lru_ba = jax.random.normal(ks[9], (DEPTH, D_LRU), f32) * 0.01
    lru_wi = jax.random.normal(ks[10], (DEPTH, LRU_BLOCKS, LRU_BLOCK_W, LRU_BLOCK_W), f32) * LRU_BLOCK_W ** -0.5
    lru_bi = jax.random.normal(ks[11], (DEPTH, D_LRU), f32) * 0.01
    a0 = jax.random.uniform(ks[12], (DEPTH, D_LRU), f32, minval=0.9, maxval=0.999)
    lru_lambda = jnp.log(a0) - jnp.log1p(-a0)
    q_norm_g = 1.0 + 0.02 * jax.random.normal(ks[13], (DEPTH, HEAD_DIM), f32)
    k_norm_g = 1.0 + 0.02 * jax.random.normal(ks[14], (DEPTH, HEAD_DIM), f32)
    w_out = jax.random.normal(ks[15], (DEPTH, D_MIX, D_MODEL), f32) * D_MIX ** -0.5
    return {"x": x, "c": c, "w_ada": w_ada, "b_ada": b_ada, "norm_g": norm_g, "w_in": w_in,
            "conv_w": conv_w, "conv_b": conv_b, "lru_wa": lru_wa, "lru_ba": lru_ba,
            "lru_wi": lru_wi, "lru_bi": lru_bi, "lru_lambda": lru_lambda,
            "q_norm_g": q_norm_g, "k_norm_g": k_norm_g, "w_out": w_out}


def reference(x, c, w_ada, b_ada, norm_g, w_in, conv_w, conv_b, lru_wa, lru_ba,
              lru_wi, lru_bi, lru_lambda, q_norm_g, k_norm_g, w_out):
    for l in range(DEPTH):
        x = hybrid_layer(x, c, w_ada[l], b_ada[l], norm_g[l], w_in[l], conv_w[l], conv_b[l],
                         lru_wa[l], lru_ba[l], lru_wi[l], lru_bi[l], lru_lambda[l],
                         q_norm_g[l], k_norm_g[l], w_out[l])
    return x
```

```python
import functools

import jax
import jax.numpy as jnp
from jax import lax
from jax.experimental import pallas as pl
from jax.experimental.pallas import tpu as pltpu

F32 = jnp.float32
BF16 = jnp.bfloat16

D_MODEL = 1024
D_LRU = 512
D_ATTN = 512
LRU_BLOCK_W = 64
CONV_W = 4
LRU_C = 8.0
HEAD_DIM = 64
MOBA_BLOCK = 256
MOBA_TOPK = 3
EPS = 1e-6
D_IN = 2 * D_LRU + 4 * D_ATTN

SUBLANES = 8
LANES = 128
VMEM_LIMIT_BYTES = 56 * 1024 * 1024

MASKED = -1e30


def _dot(a, b):
    return jnp.dot(a, b, preferred_element_type=F32)


def _dot_nt(a, b):
    return lax.dot_general(a, b, (((1,), (1,)), ((), ())), preferred_element_type=F32)


def _split_bf16(x):
    hi = x.astype(BF16)
    lo = (x - hi.astype(F32)).astype(BF16)
    return hi, lo


def _sigmoid(x):
    return 1.0 / (1.0 + jnp.exp(-x))


def _silu(x):
    return x * _sigmoid(x)


def _ada_kernel(c_ref, w_ref, b_ref, o_ref):
    sc = _silu(c_ref[...])
    s_hi, s_lo = _split_bf16(sc)
    w = w_ref[...]
    w_hi, w_lo = _split_bf16(w)
    acc = _dot(s_hi, w_hi) + _dot(s_lo, w_hi) + _dot(s_hi, w_lo)
    o_ref[...] = acc + b_ref[...]


def _ada(c, w_ada, b_ada):
    batch, d = c.shape
    n = w_ada.shape[1]
    bn = 768
    return pl.pallas_call(
        _ada_kernel,
        grid=(n // bn,),
        in_specs=[
            pl.BlockSpec((batch, d), lambda j: (0, 0)),
            pl.BlockSpec((d, bn), lambda j: (0, j)),
            pl.BlockSpec((1, bn), lambda j: (0, j)),
        ],
        out_specs=pl.BlockSpec((batch, bn), lambda j: (0, j)),
        out_shape=jax.ShapeDtypeStruct((batch, n), F32),
        compiler_params=pltpu.CompilerParams(
            dimension_semantics=("arbitrary",), vmem_limit_bytes=VMEM_LIMIT_BYTES),
        name="ada",
    )(c, w_ada, b_ada.reshape(1, n))


def _in_proj_kernel(x_ref, g_ref, scale_ref, shift_ref, w_ref, z_ref):
    x = x_ref[...]
    ms = jnp.mean(x * x, axis=-1, keepdims=True)
    y = x * lax.rsqrt(ms + EPS) * g_ref[...]
    h = y * (1.0 + scale_ref[...]) + shift_ref[...]
    z_ref[...] = _dot(h.astype(BF16), w_ref[...])


def _in_proj(x, norm_g, scale, shift, w_in_bf16):
    batch, seq, d = x.shape
    n = w_in_bf16.shape[1]
    tm = 512
    return pl.pallas_call(
        _in_proj_kernel,
        grid=(batch, seq // tm),
        in_specs=[
            pl.BlockSpec((None, tm, d), lambda b, i: (b, i, 0)),
            pl.BlockSpec((1, d), lambda b, i: (0, 0)),
            pl.BlockSpec((None, 1, d), lambda b, i: (b, 0, 0)),
            pl.BlockSpec((None, 1, d), lambda b, i: (b, 0, 0)),
            pl.BlockSpec((d, n), lambda b, i: (0, 0)),
        ],
        out_specs=pl.BlockSpec((None, tm, n), lambda b, i: (b, i, 0)),
        out_shape=jax.ShapeDtypeStruct((batch, seq, n), F32),
        compiler_params=pltpu.CompilerParams(
            dimension_semantics=("arbitrary", "arbitrary"),
            vmem_limit_bytes=VMEM_LIMIT_BYTES),
        name="in_proj",
    )(x, norm_g.reshape(1, d), scale.reshape(batch, 1, d), shift.reshape(batch, 1, d),
      w_in_bf16)


LRU_PAD = SUBLANES


def _softplus(x):
    return jnp.maximum(x, 0.0) + jnp.log(1.0 + jnp.exp(-jnp.abs(x)))


def _rg_lru_kernel(xl_ref, gl_ref, cw_ref, cb_ref, wa_ref, ba_ref, wi_ref, bi_ref,
                   lam_ref, o_ref, xpad_ref, a_ref, b_ref):
    seq = xl_ref.shape[0]
    xpad_ref[0:LRU_PAD, :] = jnp.zeros((LRU_PAD, LANES), F32)
    xpad_ref[LRU_PAD:LRU_PAD + seq, :] = xl_ref[...]
    xc = cb_ref[...] + jnp.zeros((seq, LANES), F32)
    for k in range(CONV_W):
        off = LRU_PAD - (CONV_W - 1) + k
        xc = xc + cw_ref[k:k + 1, :] * xpad_ref[off:off + seq, :]
    xcb = xc.astype(BF16)
    r = _sigmoid(_dot(xcb, wa_ref[...]) + ba_ref[...])
    gi = _sigmoid(_dot(xcb, wi_ref[...]) + bi_ref[...])
    log_a = (-LRU_C) * r * _softplus(-lam_ref[...])
    a = jnp.exp(log_a)
    mult = jnp.sqrt((1.0 - a) * (1.0 + a))
    t_idx = lax.broadcasted_iota(jnp.int32, (seq, LANES), 0)
    mult = jnp.where(t_idx == 0, 1.0, mult)
    bterm = mult * (gi * xc)

    row = t_idx % SUBLANES
    d = 1
    while d < SUBLANES:
        keep = row >= d
        a_sh = jnp.where(keep, pltpu.roll(a, d, axis=0), 1.0)
        b_sh = jnp.where(keep, pltpu.roll(bterm, d, axis=0), 0.0)
        bterm = a * b_sh + bterm
        a = a * a_sh
        d *= 2
    a_ref[...] = a
    b_ref[...] = bterm

    def body(i, carry):
        sl = pl.ds(pl.multiple_of(i * SUBLANES, SUBLANES), SUBLANES)
        h = b_ref[sl, :] + a_ref[sl, :] * carry
        o_ref[sl, :] = (h * _silu(gl_ref[sl, :])).astype(o_ref.dtype)
        return jnp.broadcast_to(h[SUBLANES - 1:SUBLANES, :], (SUBLANES, LANES))

    lax.fori_loop(0, seq // SUBLANES, body, jnp.zeros((SUBLANES, LANES), F32), unroll=8)


def _rg_lru(z, conv_w, conv_b, wa2, ba, wi2, bi, lam):
    batch, seq, _ = z.shape
    ngrp = D_LRU // LANES
    gl_off = D_LRU // LANES
    row_spec = pl.BlockSpec((1, LANES), lambda b, j: (0, j))
    return pl.pallas_call(
        _rg_lru_kernel,
        grid=(batch, ngrp),
        in_specs=[
            pl.BlockSpec((None, seq, LANES), lambda b, j: (b, 0, j)),
            pl.BlockSpec((None, seq, LANES), lambda b, j: (b, 0, gl_off + j)),
            pl.BlockSpec((CONV_W, LANES), lambda b, j: (0, j)),
            row_spec,
            pl.BlockSpec((None, LANES, LANES), lambda b, j: (j, 0, 0)),
            row_spec,
            pl.BlockSpec((None, LANES, LANES), lambda b, j: (j, 0, 0)),
            row_spec,
            row_spec,
        ],
        out_specs=pl.BlockSpec((None, seq, LANES), lambda b, j: (b, 0, j)),
        out_shape=jax.ShapeDtypeStruct((batch, seq, D_LRU), BF16),
        scratch_shapes=[
            pltpu.VMEM((LRU_PAD + seq, LANES), F32),
            pltpu.VMEM((seq, LANES), F32),
            pltpu.VMEM((seq, LANES), F32),
        ],
        compiler_params=pltpu.CompilerParams(
            dimension_semantics=("arbitrary", "arbitrary"),
            vmem_limit_bytes=VMEM_LIMIT_BYTES),
        name="rg_lru",
    )(z, z, conv_w, conv_b.reshape(1, D_LRU), wa2, ba.reshape(1, D_LRU), wi2,
      bi.reshape(1, D_LRU), lam.reshape(1, D_LRU))


HEADS_PER_CELL = LANES // HEAD_DIM


def _head_rms_norm(x, g, seg_bf16):
    x2 = x * x
    hi, lo = _split_bf16(x2)
    ms = _dot(hi, seg_bf16) + _dot(lo, seg_bf16)
    return x * lax.rsqrt(ms + EPS) * g


def _moba_kernel(q_ref, k_ref, v_ref, ga_ref, gq_ref, gk_ref, o_ref,
                 qs_ref, kh_ref, vt_ref, bias_ref, s_ref, ot_ref):
    seq = q_ref.shape[0]
    nb = seq // MOBA_BLOCK
    blk = MOBA_BLOCK

    rr = lax.broadcasted_iota(jnp.int32, (LANES, LANES), 0) // HEAD_DIM
    cc = lax.broadcasted_iota(jnp.int32, (LANES, LANES), 1) // HEAD_DIM
    seg = jnp.where(rr == cc, 1.0 / HEAD_DIM, 0.0).astype(BF16)

    qn = _head_rms_norm(q_ref[...], gq_ref[...], seg)
    kn = _head_rms_norm(k_ref[...], gk_ref[...], seg)
    qs_ref[...] = (qn * (HEAD_DIM ** -0.5)).astype(BF16)
    lane = lax.broadcasted_iota(jnp.int32, (seq, LANES), 1)
    for h in range(HEADS_PER_CELL):
        in_head = (lane // HEAD_DIM) == h
        kh_ref[h] = jnp.where(in_head, kn, 0.0).astype(BF16)
    vt_ref[...] = v_ref[...].T.astype(BF16)

    kmean = jnp.concatenate(
        [jnp.mean(kn[n * blk:(n + 1) * blk, :], axis=0, keepdims=True) for n in range(nb)],
        axis=0)
    lane_nb = lax.broadcasted_iota(jnp.int32, (nb, LANES), 1) // HEAD_DIM
    gmat = jnp.concatenate(
        [jnp.where(lane_nb == h, kmean, 0.0) for h in range(HEADS_PER_CELL)], axis=0)
    g_hi, g_lo = _split_bf16(gmat)
    q_hi, q_lo = _split_bf16(qn)
    gate = _dot_nt(g_hi, q_hi) + _dot_nt(g_lo, q_hi) + _dot_nt(g_hi, q_lo)

    blk_row = lax.broadcasted_iota(jnp.int32, (nb, blk), 0)
    for h in range(HEADS_PER_CELL):
        for qb in range(nb):
            g = gate[h * nb:(h + 1) * nb, qb * blk:(qb + 1) * blk]
            past = blk_row < qb
            if qb > MOBA_TOPK:
                cnt = jnp.zeros((nb, blk), jnp.int32)
                for m in range(qb):
                    gm = g[m:m + 1, :]
                    beats = (gm > g) | ((gm == g) & (m < blk_row))
                    cnt = cnt + beats.astype(jnp.int32)
                chosen = past & (cnt < MOBA_TOPK)
            else:
                chosen = past
            bias_ref[h, :, qb * blk:(qb + 1) * blk] = jnp.where(chosen, 0.0, MASKED)

    k_pos = lax.broadcasted_iota(jnp.int32, (blk, blk), 0)
    q_pos = lax.broadcasted_iota(jnp.int32, (blk, blk), 1)
    causal = k_pos <= q_pos
    for h in range(HEADS_PER_CELL):
        for qb in range(nb):
            q_blk = qs_ref[qb * blk:(qb + 1) * blk, :]
            m_run = None
            for kb in range(qb + 1):
                s = _dot_nt(kh_ref[h, kb * blk:(kb + 1) * blk, :], q_blk)
                if kb == qb:
                    s = jnp.where(causal, s, MASKED)
                else:
                    s = s + bias_ref[h, kb:kb + 1, qb * blk:(qb + 1) * blk]
                s_ref[kb * blk:(kb + 1) * blk, :] = s
                m_blk = jnp.max(s, axis=0, keepdims=True)
                m_run = m_blk if m_run is None else jnp.maximum(m_run, m_blk)
            n_keys = (qb + 1) * blk
            p = jnp.exp(s_ref[0:n_keys, :] - m_run)
            l_sum = jnp.sum(p, axis=0, keepdims=True)
            o_t = _dot(vt_ref[h * HEAD_DIM:(h + 1) * HEAD_DIM, 0:n_keys], p.astype(BF16))
            ot_ref[h * HEAD_DIM:(h + 1) * HEAD_DIM, qb * blk:(qb + 1) * blk] = o_t / l_sum

    o_ref[...] = (ot_ref[...].T * _silu(ga_ref[...])).astype(o_ref.dtype)


def _moba(z, gq2, gk2):
    batch, seq, _ = z.shape
    ncell = D_ATTN // LANES
    nb = seq // MOBA_BLOCK
    q_off = 2 * D_LRU // LANES
    k_off = q_off + ncell
    v_off = k_off + ncell
    ga_off = v_off + ncell

    def col_spec(off):
        return pl.BlockSpec((None, seq, LANES), lambda b, j: (b, 0, off + j))

    gain_spec = pl.BlockSpec((1, LANES), lambda b, j: (0, 0))
    return pl.pallas_call(
        _moba_kernel,
        grid=(batch, ncell),
        in_specs=[col_spec(q_off), col_spec(k_off), col_spec(v_off), col_spec(ga_off),
                  gain_spec, gain_spec],
        out_specs=pl.BlockSpec((None, seq, LANES), lambda b, j: (b, 0, j)),
        out_shape=jax.ShapeDtypeStruct((batch, seq, D_ATTN), BF16),
        scratch_shapes=[
            pltpu.VMEM((seq, LANES), BF16),
            pltpu.VMEM((HEADS_PER_CELL, seq, LANES), BF16),
            pltpu.VMEM((LANES, seq), BF16),
            pltpu.VMEM((HEADS_PER_CELL, nb, seq), F32),
            pltpu.VMEM((seq, MOBA_BLOCK), F32),
            pltpu.VMEM((LANES, seq), F32),
        ],
        compiler_params=pltpu.CompilerParams(
            dimension_semantics=("arbitrary", "arbitrary"),
            vmem_limit_bytes=VMEM_LIMIT_BYTES),
        name="moba",
    )(z, z, z, z, gq2, gk2)


def _out_proj_kernel(yl_ref, ya_ref, w_ref, x_ref, gate_ref, o_ref):
    y = _dot(yl_ref[...], w_ref[0:D_LRU, :]) + _dot(ya_ref[...], w_ref[D_LRU:, :])
    o_ref[...] = x_ref[...] + gate_ref[...] * y


def _out_proj(y_lru, y_att, w_out_bf16, x, gate):
    batch, seq, d = x.shape
    tm = 512
    return pl.pallas_call(
        _out_proj_kernel,
        grid=(batch, seq // tm),
        in_specs=[
            pl.BlockSpec((None, tm, D_LRU), lambda b, i: (b, i, 0)),
            pl.BlockSpec((None, tm, D_ATTN), lambda b, i: (b, i, 0)),
            pl.BlockSpec((D_LRU + D_ATTN, d), lambda b, i: (0, 0)),
            pl.BlockSpec((None, tm, d), lambda b, i: (b, i, 0)),
            pl.BlockSpec((None, 1, d), lambda b, i: (b, 0, 0)),
        ],
        out_specs=pl.BlockSpec((None, tm, d), lambda b, i: (b, i, 0)),
        out_shape=jax.ShapeDtypeStruct((batch, seq, d), F32),
        compiler_params=pltpu.CompilerParams(
            dimension_semantics=("arbitrary", "arbitrary"),
            vmem_limit_bytes=VMEM_LIMIT_BYTES),
        name="out_proj",
    )(y_lru, y_att, w_out_bf16, x, gate.reshape(batch, 1, d))


def _block_diag_pairs(w):
    nblk, bw, _ = w.shape
    w = w.reshape(nblk // 2, 2, bw, bw)
    zero = jnp.zeros_like(w[:, 0])
    top = jnp.concatenate([w[:, 0], zero], axis=-1)
    bot = jnp.concatenate([zero, w[:, 1]], axis=-1)
    return jnp.concatenate([top, bot], axis=-2)


def _layer(x, c, w_ada, b_ada, norm_g, w_in, conv_w, conv_b, lru_wa, lru_ba, lru_wi,
           lru_bi, lru_lambda, q_norm_g, k_norm_g, w_out):
    d = x.shape[-1]
    ada = _ada(c, w_ada, b_ada)
    shift, scale, gate = ada[:, :d], ada[:, d:2 * d], ada[:, 2 * d:]
    z = _in_proj(x, norm_g, scale, shift, w_in.astype(BF16))
    y_lru = _rg_lru(z, conv_w, conv_b, _block_diag_pairs(lru_wa).astype(BF16), lru_ba,
                    _block_diag_pairs(lru_wi).astype(BF16), lru_bi, lru_lambda)
    gq2 = jnp.tile(q_norm_g, HEADS_PER_CELL).reshape(1, LANES)
    gk2 = jnp.tile(k_norm_g, HEADS_PER_CELL).reshape(1, LANES)
    y_att = _moba(z, gq2, gk2)
    return _out_proj(y_lru, y_att, w_out.astype(BF16), x, gate)


def kernel(x, c, w_ada, b_ada, norm_g, w_in, conv_w, conv_b, lru_wa, lru_ba, lru_wi,
           lru_bi, lru_lambda, q_norm_g, k_norm_g, w_out):
    depth = w_ada.shape[0]
    for l in range(depth):
        x = _layer(x, c, w_ada[l], b_ada[l], norm_g[l], w_in[l], conv_w[l], conv_b[l],
                   lru_wa[l], lru_ba[l], lru_wi[l], lru_bi[l], lru_lambda[l],
                   q_norm_g[l], k_norm_g[l], w_out[l])
    return x
```

```python
import math

import jax
import jax.numpy as jnp
from jax import lax
from jax.experimental import pallas as pl
from jax.experimental.pallas import tpu as pltpu

F32 = jnp.float32
BF16 = jnp.bfloat16

D_MODEL = 1024
D_LRU = 512
D_ATTN = 512
LRU_BLOCK_W = 64
CONV_W = 4
LRU_C = 8.0
HEAD_DIM = 64
MOBA_BLOCK = 256
MOBA_TOPK = 3
EPS = 1e-6
D_IN = 2 * D_LRU + 4 * D_ATTN

SUBLANES = 8
LANES = 128
BF16_ROWS = 16
VMEM_LIMIT_BYTES = 56 * 1024 * 1024

LOG2E = math.log2(math.e)
MASKED = -1e30
TINY = 1e-37


def _dot(a, b):
    return jnp.dot(a, b, preferred_element_type=F32)


def _dot_nt(a, b):
    return lax.dot_general(a, b, (((1,), (1,)), ((), ())), preferred_element_type=F32)


def _split_bf16(x):
    hi = x.astype(BF16)
    lo = (x - hi.astype(F32)).astype(BF16)
    return hi, lo


def _sigmoid(x):
    return 0.5 * jnp.tanh(0.5 * x) + 0.5


def _silu(x):
    return x * _sigmoid(x)


def _ada_kernel(c_ref, w_ref, b_ref, o_ref):
    sc = _silu(c_ref[...])
    s_hi, s_lo = _split_bf16(sc)
    w = w_ref[...]
    w_hi, w_lo = _split_bf16(w)
    acc = _dot(s_hi, w_hi) + _dot(s_lo, w_hi) + _dot(s_hi, w_lo)
    o_ref[...] = acc + b_ref[...]


def _ada(c, w_ada, b_ada):
    batch, d = c.shape
    n = w_ada.shape[1]
    bn = 768
    return pl.pallas_call(
        _ada_kernel,
        grid=(n // bn,),
        in_specs=[
            pl.BlockSpec((batch, d), lambda j: (0, 0)),
            pl.BlockSpec((d, bn), lambda j: (0, j)),
            pl.BlockSpec((1, bn), lambda j: (0, j)),
        ],
        out_specs=pl.BlockSpec((batch, bn), lambda j: (0, j)),
        out_shape=jax.ShapeDtypeStruct((batch, n), F32),
        compiler_params=pltpu.CompilerParams(
            dimension_semantics=("arbitrary",), vmem_limit_bytes=VMEM_LIMIT_BYTES),
        name="ada",
    )(c, w_ada, b_ada.reshape(1, n))


def _in_proj_kernel(x_ref, g_ref, scale_ref, shift_ref, w_ref, z_ref):
    x = x_ref[...]
    ms = jnp.mean(x * x, axis=-1, keepdims=True)
    y = x * lax.rsqrt(ms + EPS) * g_ref[...]
    h = y * (1.0 + scale_ref[...]) + shift_ref[...]
    z_ref[...] = _dot(h.astype(BF16), w_ref[...])


def _in_proj(x, norm_g, scale, shift, w_in_bf16):
    batch, seq, d = x.shape
    n = w_in_bf16.shape[1]
    tm = 512
    return pl.pallas_call(
        _in_proj_kernel,
        grid=(batch, seq // tm),
        in_specs=[
            pl.BlockSpec((None, tm, d), lambda b, i: (b, i, 0)),
            pl.BlockSpec((1, d), lambda b, i: (0, 0)),
            pl.BlockSpec((None, 1, d), lambda b, i: (b, 0, 0)),
            pl.BlockSpec((None, 1, d), lambda b, i: (b, 0, 0)),
            pl.BlockSpec((d, n), lambda b, i: (0, 0)),
        ],
        out_specs=pl.BlockSpec((None, tm, n), lambda b, i: (b, i, 0)),
        out_shape=jax.ShapeDtypeStruct((batch, seq, n), F32),
        compiler_params=pltpu.CompilerParams(
            dimension_semantics=("arbitrary", "arbitrary"),
            vmem_limit_bytes=VMEM_LIMIT_BYTES),
        name="in_proj",
    )(x, norm_g.reshape(1, d), scale.reshape(batch, 1, d), shift.reshape(batch, 1, d),
      w_in_bf16)


LRU_PAD = SUBLANES
LRU_CHUNKS = 16
LRU_PITCH_PAD = SUBLANES


def _softplus(x):
    return jnp.maximum(x, 0.0) + jnp.log(1.0 + jnp.exp(-jnp.abs(x)))


def _rg_lru_kernel(xl_ref, gl_ref, cw_ref, cb_ref, wa_ref, ba_ref, wi_ref, bi_ref,
                   lam_ref, o_ref, xpad_ref, a_ref, b_ref, p_ref, h_ref):
    seq = xl_ref.shape[0]
    clen = seq // LRU_CHUNKS
    pitch = clen + LRU_PITCH_PAD
    groups = LRU_CHUNKS // SUBLANES

    xpad_ref[0:LRU_PAD, :] = jnp.zeros((LRU_PAD, LANES), F32)
    xpad_ref[LRU_PAD:LRU_PAD + seq, :] = xl_ref[...]
    xc = cb_ref[...] + jnp.zeros((seq, LANES), F32)
    for k in range(CONV_W):
        off = LRU_PAD - (CONV_W - 1) + k
        xc = xc + cw_ref[k:k + 1, :] * xpad_ref[off:off + seq, :]
    xcb = xc.astype(BF16)
    r = _sigmoid(_dot(xcb, wa_ref[...]) + ba_ref[...])
    gi = _sigmoid(_dot(xcb, wi_ref[...]) + bi_ref[...])
    decay = (-LRU_C * LOG2E) * _softplus(-lam_ref[...])
    a = jnp.exp2(r * decay)
    y = (1.0 - a) * (1.0 + a)
    mult = y * lax.rsqrt(jnp.maximum(y, TINY))
    first = lax.broadcasted_iota(jnp.int32, (SUBLANES, LANES), 0) == 0
    mult = jnp.concatenate([jnp.where(first, 1.0, mult[0:SUBLANES]), mult[SUBLANES:]], axis=0)
    bterm = mult * (gi * xc)
    for c in range(LRU_CHUNKS):
        a_ref[c * pitch:c * pitch + clen, :] = a[c * clen:(c + 1) * clen]
        b_ref[c * pitch:c * pitch + clen, :] = bterm[c * clen:(c + 1) * clen]

    def body(s, carry):
        out = []
        for g in range(groups):
            h, p = carry[g]
            idx = pl.ds(s + g * SUBLANES * pitch, SUBLANES, stride=pitch)
            a_s = a_ref[idx, :]
            h = a_s * h + b_ref[idx, :]
            p = a_s * p
            h_ref[idx, :] = h
            p_ref[idx, :] = p
            out.append((h, p))
        return tuple(out)

    init = tuple((jnp.zeros((SUBLANES, LANES), F32), jnp.ones((SUBLANES, LANES), F32))
                 for _ in range(groups))
    ends = lax.fori_loop(0, clen, body, init, unroll=8)

    carry = jnp.zeros((1, LANES), F32)
    for c in range(LRU_CHUNKS):
        rows = slice(c * pitch, c * pitch + clen)
        h = h_ref[rows, :] + p_ref[rows, :] * carry
        out_rows = slice(c * clen, (c + 1) * clen)
        o_ref[out_rows, :] = (h * _silu(gl_ref[out_rows, :])).astype(o_ref.dtype)
        h_end, p_end = ends[c // SUBLANES]
        j = c % SUBLANES
        carry = h_end[j:j + 1, :] + p_end[j:j + 1, :] * carry


def _rg_lru(z, conv_w, conv_b, wa2, ba, wi2, bi, lam):
    batch, seq, _ = z.shape
    ngrp = D_LRU // LANES
    gl_off = D_LRU // LANES
    scan_rows = LRU_CHUNKS * (seq // LRU_CHUNKS + LRU_PITCH_PAD)
    row_spec = pl.BlockSpec((1, LANES), lambda b, j: (0, j))
    return pl.pallas_call(
        _rg_lru_kernel,
        grid=(batch, ngrp),
        in_specs=[
            pl.BlockSpec((None, seq, LANES), lambda b, j: (b, 0, j)),
            pl.BlockSpec((None, seq, LANES), lambda b, j: (b, 0, gl_off + j)),
            pl.BlockSpec((CONV_W, LANES), lambda b, j: (0, j)),
            row_spec,
            pl.BlockSpec((None, LANES, LANES), lambda b, j: (j, 0, 0)),
            row_spec,
            pl.BlockSpec((None, LANES, LANES), lambda b, j: (j, 0, 0)),
            row_spec,
            row_spec,
        ],
        out_specs=pl.BlockSpec((None, seq, LANES), lambda b, j: (b, 0, j)),
        out_shape=jax.ShapeDtypeStruct((batch, seq, D_LRU), BF16),
        scratch_shapes=[
            pltpu.VMEM((LRU_PAD + seq, LANES), F32),
            pltpu.VMEM((scan_rows, LANES), F32),
            pltpu.VMEM((scan_rows, LANES), F32),
            pltpu.VMEM((scan_rows, LANES), F32),
            pltpu.VMEM((scan_rows, LANES), F32),
        ],
        compiler_params=pltpu.CompilerParams(
            dimension_semantics=("arbitrary", "arbitrary"),
            vmem_limit_bytes=VMEM_LIMIT_BYTES),
        name="rg_lru",
    )(z, z, conv_w, conv_b.reshape(1, D_LRU), wa2, ba.reshape(1, D_LRU), wi2,
      bi.reshape(1, D_LRU), lam.reshape(1, D_LRU))


HEADS_PER_CELL = LANES // HEAD_DIM
V_AUG_ROWS = HEAD_DIM + BF16_ROWS


def _head_rms_norm(x, g, seg_bf16):
    ms = _dot((x * x).astype(BF16), seg_bf16)
    return x * lax.rsqrt(ms + EPS) * g


def _moba_kernel(q_ref, k_ref, v_ref, ga_ref, gq_ref, gk_ref, o_ref,
                 qs_ref, kh_ref, va_ref, bias_ref, s_ref, ot_ref):
    seq = q_ref.shape[0]
    nb = seq // MOBA_BLOCK
    blk = MOBA_BLOCK

    rr = lax.broadcasted_iota(jnp.int32, (LANES, LANES), 0) // HEAD_DIM
    cc = lax.broadcasted_iota(jnp.int32, (LANES, LANES), 1) // HEAD_DIM
    seg = jnp.where(rr == cc, 1.0 / HEAD_DIM, 0.0).astype(BF16)

    qn = _head_rms_norm(q_ref[...], gq_ref[...], seg)
    kn = _head_rms_norm(k_ref[...], gk_ref[...], seg)
    qs_ref[...] = (qn * (HEAD_DIM ** -0.5 * LOG2E)).astype(BF16)
    lane = lax.broadcasted_iota(jnp.int32, (seq, LANES), 1)
    vt = v_ref[...].T
    for h in range(HEADS_PER_CELL):
        in_head = (lane // HEAD_DIM) == h
        kh_ref[h] = jnp.where(in_head, kn, 0.0).astype(BF16)
        va_ref[h, 0:HEAD_DIM, :] = vt[h * HEAD_DIM:(h + 1) * HEAD_DIM].astype(BF16)
        va_ref[h, HEAD_DIM:V_AUG_ROWS, :] = jnp.ones((BF16_ROWS, seq), BF16)

    kmean = jnp.concatenate(
        [jnp.mean(kn[n * blk:(n + 1) * blk, :], axis=0, keepdims=True) for n in range(nb)],
        axis=0)
    lane_nb = lax.broadcasted_iota(jnp.int32, (nb, LANES), 1) // HEAD_DIM
    gmat = jnp.concatenate(
        [jnp.where(lane_nb == h, kmean, 0.0) for h in range(HEADS_PER_CELL)], axis=0)
    first_ranked = (MOBA_TOPK + 1) * blk
    g_hi, g_lo = _split_bf16(gmat)
    q_hi, q_lo = _split_bf16(qn[first_ranked:])
    gate = _dot_nt(g_hi, q_hi) + _dot_nt(g_lo, q_hi) + _dot_nt(g_hi, q_lo)

    blk_row = lax.broadcasted_iota(jnp.int32, (nb, blk), 0)
    for h in range(HEADS_PER_CELL):
        for qb in range(nb):
            past = blk_row < qb
            if qb > MOBA_TOPK:
                col = qb * blk - first_ranked
                g = gate[h * nb:(h + 1) * nb, col:col + blk]
                cnt = jnp.zeros((nb, blk), jnp.int32)
                for m in range(qb):
                    gm = g[m:m + 1, :]
                    beats = (gm > g) | ((gm == g) & (m < blk_row))
                    cnt = cnt + beats.astype(jnp.int32)
                chosen = past & (cnt < MOBA_TOPK)
            else:
                chosen = past
            bias_ref[h, :, qb * blk:(qb + 1) * blk] = jnp.where(chosen, 0.0, MASKED)

    k_pos = lax.broadcasted_iota(jnp.int32, (blk, blk), 0)
    q_pos = lax.broadcasted_iota(jnp.int32, (blk, blk), 1)
    causal = k_pos <= q_pos

    def scores(slot, h, qb):
        q_blk = qs_ref[qb * blk:(qb + 1) * blk, :]
        m_run = None
        biases = []
        for kb in range(qb + 1):
            s = _dot_nt(kh_ref[h, kb * blk:(kb + 1) * blk, :], q_blk)
            if kb == qb:
                s = jnp.where(causal, s, MASKED)
            s_ref[slot, kb * blk:(kb + 1) * blk, :] = s
            m_blk = jnp.max(s, axis=0, keepdims=True)
            if kb == qb:
                biases.append(None)
            else:
                b_row = bias_ref[h, kb:kb + 1, qb * blk:(qb + 1) * blk]
                m_blk = m_blk + b_row
                biases.append(b_row)
            m_run = m_blk if m_run is None else jnp.maximum(m_run, m_blk)
        return [m_run if b is None else m_run - b for b in biases]

    def attend(slot, h, qb, subs):
        ps = []
        for kb in range(qb + 1):
            p = jnp.exp2(s_ref[slot, kb * blk:(kb + 1) * blk, :] - subs[kb])
            ps.append(p.astype(BF16))
        p_all = ps[0] if len(ps) == 1 else jnp.concatenate(ps, axis=0)
        n_keys = (qb + 1) * blk
        o_aug = _dot(va_ref[h, :, 0:n_keys], p_all)
        inv_l = 1.0 / o_aug[HEAD_DIM:HEAD_DIM + 1, :]
        ot_ref[h * HEAD_DIM:(h + 1) * HEAD_DIM, qb * blk:(qb + 1) * blk] = (
            o_aug[0:HEAD_DIM, :] * inv_l)

    items = [(h, qb) for h in range(HEADS_PER_CELL) for qb in range(nb)]
    subs = scores(0, *items[0])
    for i, (h, qb) in enumerate(items):
        nxt = scores((i + 1) % 2, *items[i + 1]) if i + 1 < len(items) else None
        attend(i % 2, h, qb, subs)
        subs = nxt

    o_ref[...] = (ot_ref[...].T * _silu(ga_ref[...])).astype(o_ref.dtype)


def _moba(z, gq2, gk2):
    batch, seq, _ = z.shape
    ncell = D_ATTN // LANES
    nb = seq // MOBA_BLOCK
    q_off = 2 * D_LRU // LANES
    k_off = q_off + ncell
    v_off = k_off + ncell
    ga_off = v_off + ncell

    def col_spec(off):
        return pl.BlockSpec((None, seq, LANES), lambda b, j: (b, 0, off + j))

    gain_spec = pl.BlockSpec((1, LANES), lambda b, j: (0, 0))
    return pl.pallas_call(
        _moba_kernel,
        grid=(batch, ncell),
        in_specs=[col_spec(q_off), col_spec(k_off), col_spec(v_off), col_spec(ga_off),
                  gain_spec, gain_spec],
        out_specs=pl.BlockSpec((None, seq, LANES), lambda b, j: (b, 0, j)),
        out_shape=jax.ShapeDtypeStruct((batch, seq, D_ATTN), BF16),
        scratch_shapes=[
            pltpu.VMEM((seq, LANES), BF16),
            pltpu.VMEM((HEADS_PER_CELL, seq, LANES), BF16),
            pltpu.VMEM((HEADS_PER_CELL, V_AUG_ROWS, seq), BF16),
            pltpu.VMEM((HEADS_PER_CELL, nb, seq), F32),
            pltpu.VMEM((2, seq, MOBA_BLOCK), F32),
            pltpu.VMEM((LANES, seq), F32),
        ],
        compiler_params=pltpu.CompilerParams(
            dimension_semantics=("arbitrary", "arbitrary"),
            vmem_limit_bytes=VMEM_LIMIT_BYTES),
        name="moba",
    )(z, z, z, z, gq2, gk2)


def _out_proj_kernel(yl_ref, ya_ref, w_ref, x_ref, gate_ref, o_ref):
    y = _dot(yl_ref[...], w_ref[0:D_LRU, :]) + _dot(ya_ref[...], w_ref[D_LRU:, :])
    o_ref[...] = x_ref[...] + gate_ref[...] * y


def _out_proj(y_lru, y_att, w_out_bf16, x, gate):
    batch, seq, d = x.shape
    tm = 512
    return pl.pallas_call(
        _out_proj_kernel,
        grid=(batch, seq // tm),
        in_specs=[
            pl.BlockSpec((None, tm, D_LRU), lambda b, i: (b, i, 0)),
            pl.BlockSpec((None, tm, D_ATTN), lambda b, i: (b, i, 0)),
            pl.BlockSpec((D_LRU + D_ATTN, d), lambda b, i: (0, 0)),
            pl.BlockSpec((None, tm, d), lambda b, i: (b, i, 0)),
            pl.BlockSpec((None, 1, d), lambda b, i: (b, 0, 0)),
        ],
        out_specs=pl.BlockSpec((None, tm, d), lambda b, i: (b, i, 0)),
        out_shape=jax.ShapeDtypeStruct((batch, seq, d), F32),
        compiler_params=pltpu.CompilerParams(
            dimension_semantics=("arbitrary", "arbitrary"),
            vmem_limit_bytes=VMEM_LIMIT_BYTES),
        name="out_proj",
    )(y_lru, y_att, w_out_bf16, x, gate.reshape(batch, 1, d))


def _block_diag_pairs(w):
    nblk, bw, _ = w.shape
    w = w.reshape(nblk // 2, 2, bw, bw)
    zero = jnp.zeros_like(w[:, 0])
    top = jnp.concatenate([w[:, 0], zero], axis=-1)
    bot = jnp.concatenate([zero, w[:, 1]], axis=-1)
    return jnp.concatenate([top, bot], axis=-2)


def _layer(x, c, w_ada, b_ada, norm_g, w_in, conv_w, conv_b, lru_wa, lru_ba, lru_wi,
           lru_bi, lru_lambda, q_norm_g, k_norm_g, w_out):
    d = x.shape[-1]
    ada = _ada(c, w_ada, b_ada)
    shift, scale, gate = ada[:, :d], ada[:, d:2 * d], ada[:, 2 * d:]
    z = _in_proj(x, norm_g, scale, shift, w_in.astype(BF16))
    y_lru = _rg_lru(z, conv_w, conv_b, _block_diag_pairs(lru_wa).astype(BF16), lru_ba,
                    _block_diag_pairs(lru_wi).astype(BF16), lru_bi, lru_lambda)
    gq2 = jnp.tile(q_norm_g, HEADS_PER_CELL).reshape(1, LANES)
    gk2 = jnp.tile(k_norm_g, HEADS_PER_CELL).reshape(1, LANES)
    y_att = _moba(z, gq2, gk2)
    return _out_proj(y_lru, y_att, w_out.astype(BF16), x, gate)


def kernel(x, c, w_ada, b_ada, norm_g, w_in, conv_w, conv_b, lru_wa, lru_ba, lru_wi,
           lru_bi, lru_lambda, q_norm_g, k_norm_g, w_out):
    depth = w_ada.shape[0]
    for l in range(depth):
        x = _layer(x, c, w_ada[l], b_ada[l], norm_g[l], w_in[l], conv_w[l], conv_b[l],
                   lru_wa[l], lru_ba[l], lru_wi[l], lru_bi[l], lru_lambda[l],
                   q_norm_g[l], k_norm_g[l], w_out[l])
    return x
```

```python
import math

import jax
import jax.numpy as jnp
from jax import lax
from jax.experimental import pallas as pl
from jax.experimental.pallas import tpu as pltpu

F32 = jnp.float32
BF16 = jnp.bfloat16

D_MODEL = 1024
D_LRU = 512
D_ATTN = 512
LRU_BLOCK_W = 64
CONV_W = 4
LRU_C = 8.0
HEAD_DIM = 64
MOBA_BLOCK = 256
MOBA_TOPK = 3
EPS = 1e-6
D_IN = 2 * D_LRU + 4 * D_ATTN

SUBLANES = 8
LANES = 128
BF16_ROWS = 16
VMEM_LIMIT_BYTES = 56 * 1024 * 1024
IN_PROJ_ROWS = 1024
OUT_PROJ_ROWS = 1024

LOG2E = math.log2(math.e)
MASKED = -1e30
TINY = 1e-37


def _dot(a, b):
    return jnp.dot(a, b, preferred_element_type=F32)


def _dot_nt(a, b):
    return lax.dot_general(a, b, (((1,), (1,)), ((), ())), preferred_element_type=F32)


def _split_bf16(x):
    hi = x.astype(BF16)
    lo = (x - hi.astype(F32)).astype(BF16)
    return hi, lo


def _silu(x):
    hx = 0.5 * x
    return hx * (1.0 + jnp.tanh(hx))


def _ada_kernel(c_ref, w_ref, b_ref, o_ref):
    sc = _silu(c_ref[...])
    s_hi, s_lo = _split_bf16(sc)
    w = w_ref[...]
    w_hi, w_lo = _split_bf16(w)
    acc = _dot(s_hi, w_hi) + _dot(s_lo, w_hi) + _dot(s_hi, w_lo)
    o_ref[...] = acc + b_ref[...]


def _ada(c, w_ada, b_ada):
    batch, d = c.shape
    n = w_ada.shape[1]
    bn = 768
    return pl.pallas_call(
        _ada_kernel,
        grid=(n // bn,),
        in_specs=[
            pl.BlockSpec((batch, d), lambda j: (0, 0)),
            pl.BlockSpec((d, bn), lambda j: (0, j)),
            pl.BlockSpec((1, bn), lambda j: (0, j)),
        ],
        out_specs=pl.BlockSpec((batch, bn), lambda j: (0, j)),
        out_shape=jax.ShapeDtypeStruct((batch, n), F32),
        compiler_params=pltpu.CompilerParams(
            dimension_semantics=("arbitrary",), vmem_limit_bytes=VMEM_LIMIT_BYTES),
        name="ada",
    )(c, w_ada, b_ada.reshape(1, n))


def _in_proj_kernel(x_ref, g_ref, scale_ref, shift_ref, w_ref, z_ref):
    x = x_ref[...]
    ms = jnp.mean(x * x, axis=-1, keepdims=True)
    y = x * lax.rsqrt(ms + EPS) * g_ref[...]
    h = y * (1.0 + scale_ref[...]) + shift_ref[...]
    z_ref[...] = _dot(h.astype(BF16), w_ref[...])


def _in_proj(x, norm_g, scale, shift, w_in_bf16):
    batch, seq, d = x.shape
    n = w_in_bf16.shape[1]
    tm = IN_PROJ_ROWS
    return pl.pallas_call(
        _in_proj_kernel,
        grid=(batch, seq // tm),
        in_specs=[
            pl.BlockSpec((None, tm, d), lambda b, i: (b, i, 0)),
            pl.BlockSpec((1, d), lambda b, i: (0, 0)),
            pl.BlockSpec((None, 1, d), lambda b, i: (b, 0, 0)),
            pl.BlockSpec((None, 1, d), lambda b, i: (b, 0, 0)),
            pl.BlockSpec((d, n), lambda b, i: (0, 0)),
        ],
        out_specs=pl.BlockSpec((None, tm, n), lambda b, i: (b, i, 0)),
        out_shape=jax.ShapeDtypeStruct((batch, seq, n), F32),
        compiler_params=pltpu.CompilerParams(
            dimension_semantics=("arbitrary", "arbitrary"),
            vmem_limit_bytes=VMEM_LIMIT_BYTES),
        name="in_proj",
    )(x, norm_g.reshape(1, d), scale.reshape(batch, 1, d), shift.reshape(batch, 1, d),
      w_in_bf16)


LRU_PAD = SUBLANES
LRU_CHUNKS = 16
LRU_PITCH_PAD = SUBLANES


def _softplus(x):
    return jnp.maximum(x, 0.0) + jnp.log(1.0 + jnp.exp(-jnp.abs(x)))


def _rg_lru_kernel(xl_ref, gl_ref, cw_ref, cb_ref, wa_ref, ba_ref, wi_ref, bi_ref,
                   lam_ref, o_ref, xpad_ref, a_ref, b_ref, p_ref, h_ref):
    seq = xl_ref.shape[0]
    clen = seq // LRU_CHUNKS
    pitch = clen + LRU_PITCH_PAD
    groups = LRU_CHUNKS // SUBLANES

    xpad_ref[0:LRU_PAD, :] = jnp.zeros((LRU_PAD, LANES), F32)
    xpad_ref[LRU_PAD:LRU_PAD + seq, :] = xl_ref[...]
    xc = None
    for k in range(CONV_W):
        off = LRU_PAD - (CONV_W - 1) + k
        tap = cw_ref[k:k + 1, :] * xpad_ref[off:off + seq, :]
        xc = tap if xc is None else xc + tap
    xc = xc + cb_ref[...]
    xcb = xc.astype(BF16)
    t_r = jnp.tanh(_dot(xcb, wa_ref[...] * 0.5) + 0.5 * ba_ref[...])
    t_i = jnp.tanh(_dot(xcb, wi_ref[...] * 0.5) + 0.5 * bi_ref[...])
    half_decay = (-0.5 * LRU_C * LOG2E) * _softplus(-lam_ref[...])
    a = jnp.exp2(t_r * half_decay + half_decay)
    y = 1.0 - a * a
    mult = y * lax.rsqrt(jnp.maximum(y, TINY))
    first = lax.broadcasted_iota(jnp.int32, (SUBLANES, LANES), 0) == 0
    mult = jnp.concatenate([jnp.where(first, 1.0, mult[0:SUBLANES]), mult[SUBLANES:]], axis=0)
    bterm = (mult * xc) * (0.5 * t_i + 0.5)
    for c in range(LRU_CHUNKS):
        a_ref[c * pitch:c * pitch + clen, :] = a[c * clen:(c + 1) * clen]
        b_ref[c * pitch:c * pitch + clen, :] = bterm[c * clen:(c + 1) * clen]

    def body(s, carry):
        out = []
        for g in range(groups):
            h, p = carry[g]
            idx = pl.ds(s + g * SUBLANES * pitch, SUBLANES, stride=pitch)
            a_s = a_ref[idx, :]
            h = a_s * h + b_ref[idx, :]
            p = a_s * p
            h_ref[idx, :] = h
            p_ref[idx, :] = p
            out.append((h, p))
        return tuple(out)

    init = tuple((jnp.zeros((SUBLANES, LANES), F32), jnp.ones((SUBLANES, LANES), F32))
                 for _ in range(groups))
    ends = lax.fori_loop(0, clen, body, init, unroll=8)

    carry = jnp.zeros((1, LANES), F32)
    for c in range(LRU_CHUNKS):
        rows = slice(c * pitch, c * pitch + clen)
        h = h_ref[rows, :] + p_ref[rows, :] * carry
        out_rows = slice(c * clen, (c + 1) * clen)
        o_ref[out_rows, :] = (h * _silu(gl_ref[out_rows, :])).astype(o_ref.dtype)
        h_end, p_end = ends[c // SUBLANES]
        j = c % SUBLANES
        carry = h_end[j:j + 1, :] + p_end[j:j + 1, :] * carry


def _rg_lru(z, conv_w, conv_b, wa2, ba, wi2, bi, lam):
    batch, seq, _ = z.shape
    ngrp = D_LRU // LANES
    gl_off = D_LRU // LANES
    scan_rows = LRU_CHUNKS * (seq // LRU_CHUNKS + LRU_PITCH_PAD)
    row_spec = pl.BlockSpec((1, LANES), lambda b, j: (0, j))
    return pl.pallas_call(
        _rg_lru_kernel,
        grid=(batch, ngrp),
        in_specs=[
            pl.BlockSpec((None, seq, LANES), lambda b, j: (b, 0, j)),
            pl.BlockSpec((None, seq, LANES), lambda b, j: (b, 0, gl_off + j)),
            pl.BlockSpec((CONV_W, LANES), lambda b, j: (0, j)),
            row_spec,
            pl.BlockSpec((None, LANES, LANES), lambda b, j: (j, 0, 0)),
            row_spec,
            pl.BlockSpec((None, LANES, LANES), lambda b, j: (j, 0, 0)),
            row_spec,
            row_spec,
        ],
        out_specs=pl.BlockSpec((None, seq, LANES), lambda b, j: (b, 0, j)),
        out_shape=jax.ShapeDtypeStruct((batch, seq, D_LRU), BF16),
        scratch_shapes=[
            pltpu.VMEM((LRU_PAD + seq, LANES), F32),
            pltpu.VMEM((scan_rows, LANES), F32),
            pltpu.VMEM((scan_rows, LANES), F32),
            pltpu.VMEM((scan_rows, LANES), F32),
            pltpu.VMEM((scan_rows, LANES), F32),
        ],
        compiler_params=pltpu.CompilerParams(
            dimension_semantics=("arbitrary", "arbitrary"),
            vmem_limit_bytes=VMEM_LIMIT_BYTES),
        name="rg_lru",
    )(z, z, conv_w, conv_b.reshape(1, D_LRU), wa2, ba.reshape(1, D_LRU), wi2,
      bi.reshape(1, D_LRU), lam.reshape(1, D_LRU))


HEADS_PER_CELL = LANES // HEAD_DIM
V_AUG_ROWS = HEAD_DIM + BF16_ROWS
SCORE_LOOKAHEAD = 2
SCORE_SLOTS = SCORE_LOOKAHEAD + 1


def _head_rms_norm(x, g, seg_bf16):
    ms = _dot((x * x).astype(BF16), seg_bf16)
    return x * lax.rsqrt(ms + EPS) * g


def _moba_kernel(q_ref, k_ref, v_ref, ga_ref, gq_ref, gk_ref, o_ref,
                 qs_ref, kh_ref, va_ref, bias_ref, s_ref, ot_ref):
    seq = q_ref.shape[0]
    nb = seq // MOBA_BLOCK
    blk = MOBA_BLOCK

    rr = lax.broadcasted_iota(jnp.int32, (LANES, LANES), 0) // HEAD_DIM
    cc = lax.broadcasted_iota(jnp.int32, (LANES, LANES), 1) // HEAD_DIM
    seg = jnp.where(rr == cc, 1.0 / HEAD_DIM, 0.0).astype(BF16)

    qn = _head_rms_norm(q_ref[...], gq_ref[...], seg)
    kn = _head_rms_norm(k_ref[...], gk_ref[...], seg)
    qs_ref[...] = (qn * (HEAD_DIM ** -0.5 * LOG2E)).astype(BF16)
    lane = lax.broadcasted_iota(jnp.int32, (seq, LANES), 1)
    vt = v_ref[...].T
    for h in range(HEADS_PER_CELL):
        in_head = (lane // HEAD_DIM) == h
        kh_ref[h] = jnp.where(in_head, kn, 0.0).astype(BF16)
        va_ref[h, 0:HEAD_DIM, :] = vt[h * HEAD_DIM:(h + 1) * HEAD_DIM].astype(BF16)
        va_ref[h, HEAD_DIM:V_AUG_ROWS, :] = jnp.ones((BF16_ROWS, seq), BF16)

    kmean = jnp.concatenate(
        [jnp.mean(kn[n * blk:(n + 1) * blk, :], axis=0, keepdims=True) for n in range(nb)],
        axis=0)
    lane_nb = lax.broadcasted_iota(jnp.int32, (nb, LANES), 1) // HEAD_DIM
    gmat = jnp.concatenate(
        [jnp.where(lane_nb == h, kmean, 0.0) for h in range(HEADS_PER_CELL)], axis=0)
    first_ranked = (MOBA_TOPK + 1) * blk
    g_hi, g_lo = _split_bf16(gmat)
    q_hi, q_lo = _split_bf16(qn[first_ranked:])
    gate = _dot_nt(g_hi, q_hi) + _dot_nt(g_lo, q_hi) + _dot_nt(g_hi, q_lo)

    blk_row = lax.broadcasted_iota(jnp.int32, (nb, blk), 0)
    for h in range(HEADS_PER_CELL):
        for qb in range(nb):
            past = blk_row < qb
            if qb > MOBA_TOPK:
                col = qb * blk - first_ranked
                g = gate[h * nb:(h + 1) * nb, col:col + blk]
                cnt = jnp.zeros((nb, blk), jnp.int32)
                for m in range(qb):
                    gm = g[m:m + 1, :]
                    beats = (gm > g) | ((gm == g) & (m < blk_row))
                    cnt = cnt + beats.astype(jnp.int32)
                chosen = past & (cnt < MOBA_TOPK)
            else:
                chosen = past
            bias_ref[h, :, qb * blk:(qb + 1) * blk] = jnp.where(chosen, 0.0, MASKED)

    k_pos = lax.broadcasted_iota(jnp.int32, (blk, blk), 0)
    q_pos = lax.broadcasted_iota(jnp.int32, (blk, blk), 1)
    causal = k_pos <= q_pos

    def scores(slot, h, qb):
        q_blk = qs_ref[qb * blk:(qb + 1) * blk, :]
        m_run = None
        biases = []
        for kb in range(qb + 1):
            s = _dot_nt(kh_ref[h, kb * blk:(kb + 1) * blk, :], q_blk)
            if kb == qb:
                s = jnp.where(causal, s, MASKED)
            s_ref[slot, kb * blk:(kb + 1) * blk, :] = s
            m_blk = jnp.max(s, axis=0, keepdims=True)
            if kb == qb:
                biases.append(None)
            else:
                b_row = bias_ref[h, kb:kb + 1, qb * blk:(qb + 1) * blk]
                m_blk = m_blk + b_row
                biases.append(b_row)
            m_run = m_blk if m_run is None else jnp.maximum(m_run, m_blk)
        return [m_run if b is None else m_run - b for b in biases]

    def attend(slot, h, qb, subs):
        ps = []
        for kb in range(qb + 1):
            p = jnp.exp2(s_ref[slot, kb * blk:(kb + 1) * blk, :] - subs[kb])
            ps.append(p.astype(BF16))
        p_all = ps[0] if len(ps) == 1 else jnp.concatenate(ps, axis=0)
        n_keys = (qb + 1) * blk
        o_aug = _dot(va_ref[h, :, 0:n_keys], p_all)
        inv_l = 1.0 / o_aug[HEAD_DIM:HEAD_DIM + 1, :]
        ot_ref[h * HEAD_DIM:(h + 1) * HEAD_DIM, qb * blk:(qb + 1) * blk] = (
            o_aug[0:HEAD_DIM, :] * inv_l)

    items = [(h, qb) for h in range(HEADS_PER_CELL) for qb in range(nb)]
    pending = {}
    for i in range(min(SCORE_LOOKAHEAD, len(items))):
        pending[i] = scores(i % SCORE_SLOTS, *items[i])
    for i, (h, qb) in enumerate(items):
        j = i + SCORE_LOOKAHEAD
        if j < len(items):
            pending[j] = scores(j % SCORE_SLOTS, *items[j])
        attend(i % SCORE_SLOTS, h, qb, pending.pop(i))

    o_ref[...] = (ot_ref[...].T * _silu(ga_ref[...])).astype(o_ref.dtype)


def _moba(z, gq2, gk2):
    batch, seq, _ = z.shape
    ncell = D_ATTN // LANES
    nb = seq // MOBA_BLOCK
    q_off = 2 * D_LRU // LANES
    k_off = q_off + ncell
    v_off = k_off + ncell
    ga_off = v_off + ncell

    def col_spec(off):
        return pl.BlockSpec((None, seq, LANES), lambda b, j: (b, 0, off + j))

    gain_spec = pl.BlockSpec((1, LANES), lambda b, j: (0, 0))
    return pl.pallas_call(
        _moba_kernel,
        grid=(batch, ncell),
        in_specs=[col_spec(q_off), col_spec(k_off), col_spec(v_off), col_spec(ga_off),
                  gain_spec, gain_spec],
        out_specs=pl.BlockSpec((None, seq, LANES), lambda b, j: (b, 0, j)),
        out_shape=jax.ShapeDtypeStruct((batch, seq, D_ATTN), BF16),
        scratch_shapes=[
            pltpu.VMEM((seq, LANES), BF16),
            pltpu.VMEM((HEADS_PER_CELL, seq, LANES), BF16),
            pltpu.VMEM((HEADS_PER_CELL, V_AUG_ROWS, seq), BF16),
            pltpu.VMEM((HEADS_PER_CELL, nb, seq), F32),
            pltpu.VMEM((SCORE_SLOTS, seq, MOBA_BLOCK), F32),
            pltpu.VMEM((LANES, seq), F32),
        ],
        compiler_params=pltpu.CompilerParams(
            dimension_semantics=("arbitrary", "arbitrary"),
            vmem_limit_bytes=VMEM_LIMIT_BYTES),
        name="moba",
    )(z, z, z, z, gq2, gk2)


def _out_proj_kernel(yl_ref, ya_ref, w_ref, x_ref, gate_ref, o_ref):
    y = _dot(yl_ref[...], w_ref[0:D_LRU, :]) + _dot(ya_ref[...], w_ref[D_LRU:, :])
    o_ref[...] = x_ref[...] + gate_ref[...] * y


def _out_proj(y_lru, y_att, w_out_bf16, x, gate):
    batch, seq, d = x.shape
    tm = OUT_PROJ_ROWS
    return pl.pallas_call(
        _out_proj_kernel,
        grid=(batch, seq // tm),
        in_specs=[
            pl.BlockSpec((None, tm, D_LRU), lambda b, i: (b, i, 0)),
            pl.BlockSpec((None, tm, D_ATTN), lambda b, i: (b, i, 0)),
            pl.BlockSpec((D_LRU + D_ATTN, d), lambda b, i: (0, 0)),
            pl.BlockSpec((None, tm, d), lambda b, i: (b, i, 0)),
            pl.BlockSpec((None, 1, d), lambda b, i: (b, 0, 0)),
        ],
        out_specs=pl.BlockSpec((None, tm, d), lambda b, i: (b, i, 0)),
        out_shape=jax.ShapeDtypeStruct((batch, seq, d), F32),
        compiler_params=pltpu.CompilerParams(
            dimension_semantics=("arbitrary", "arbitrary"),
            vmem_limit_bytes=VMEM_LIMIT_BYTES),
        name="out_proj",
    )(y_lru, y_att, w_out_bf16, x, gate.reshape(batch, 1, d))


def _block_diag_pairs(w):
    nblk, bw, _ = w.shape
    w = w.reshape(nblk // 2, 2, bw, bw)
    zero = jnp.zeros_like(w[:, 0])
    top = jnp.concatenate([w[:, 0], zero], axis=-1)
    bot = jnp.concatenate([zero, w[:, 1]], axis=-1)
    return jnp.concatenate([top, bot], axis=-2)


def _layer(x, c, w_ada, b_ada, norm_g, w_in, conv_w, conv_b, lru_wa, lru_ba, lru_wi,
           lru_bi, lru_lambda, q_norm_g, k_norm_g, w_out):
    d = x.shape[-1]
    ada = _ada(c, w_ada, b_ada)
    shift, scale, gate = ada[:, :d], ada[:, d:2 * d], ada[:, 2 * d:]
    z = _in_proj(x, norm_g, scale, shift, w_in.astype(BF16))
    y_lru = _rg_lru(z, conv_w, conv_b, _block_diag_pairs(lru_wa).astype(BF16), lru_ba,
                    _block_diag_pairs(lru_wi).astype(BF16), lru_bi, lru_lambda)
    gq2 = jnp.tile(q_norm_g, HEADS_PER_CELL).reshape(1, LANES)
    gk2 = jnp.tile(k_norm_g, HEADS_PER_CELL).reshape(1, LANES)
    y_att = _moba(z, gq2, gk2)
    return _out_proj(y_lru, y_att, w_out.astype(BF16), x, gate)


def kernel(x, c, w_ada, b_ada, norm_g, w_in, conv_w, conv_b, lru_wa, lru_ba, lru_wi,
           lru_bi, lru_lambda, q_norm_g, k_norm_g, w_out):
    depth = w_ada.shape[0]
    for l in range(depth):
        x = _layer(x, c, w_ada[l], b_ada[l], norm_g[l], w_in[l], conv_w[l], conv_b[l],
                   lru_wa[l], lru_ba[l], lru_wi[l], lru_bi[l], lru_lambda[l],
                   q_norm_g[l], k_norm_g[l], w_out[l])
    return x
```

```python
import math

import jax
import jax.numpy as jnp
from jax import lax
from jax.experimental import pallas as pl
from jax.experimental.pallas import tpu as pltpu

F32 = jnp.float32
BF16 = jnp.bfloat16

D_MODEL = 1024
D_LRU = 512
D_ATTN = 512
LRU_BLOCK_W = 64
CONV_W = 4
LRU_C = 8.0
HEAD_DIM = 64
MOBA_BLOCK = 256
MOBA_TOPK = 3
EPS = 1e-6
D_IN = 2 * D_LRU + 4 * D_ATTN

SUBLANES = 8
LANES = 128
BF16_ROWS = 16
VMEM_LIMIT_BYTES = 56 * 1024 * 1024
IN_PROJ_ROWS = 1024
OUT_PROJ_ROWS = 1024

LOG2E = math.log2(math.e)
MASKED = -1e30
TINY = 1e-37


def _dot(a, b):
    return jnp.dot(a, b, preferred_element_type=F32)


def _dot_nt(a, b):
    return lax.dot_general(a, b, (((1,), (1,)), ((), ())), preferred_element_type=F32)


def _split_bf16(x):
    hi = x.astype(BF16)
    lo = (x - hi.astype(F32)).astype(BF16)
    return hi, lo


def _silu(x):
    hx = 0.5 * x
    return hx * (1.0 + jnp.tanh(hx))


def _ada_kernel(c_ref, w_ref, b_ref, o_ref):
    sc = _silu(c_ref[...])
    s_hi, s_lo = _split_bf16(sc)
    w = w_ref[...]
    w_hi, w_lo = _split_bf16(w)
    acc = _dot(s_hi, w_hi) + _dot(s_lo, w_hi) + _dot(s_hi, w_lo)
    o_ref[...] = acc + b_ref[...]


def _ada(c, w_ada, b_ada):
    batch, d = c.shape
    n = w_ada.shape[1]
    bn = 768
    return pl.pallas_call(
        _ada_kernel,
        grid=(n // bn,),
        in_specs=[
            pl.BlockSpec((batch, d), lambda j: (0, 0)),
            pl.BlockSpec((d, bn), lambda j: (0, j)),
            pl.BlockSpec((1, bn), lambda j: (0, j)),
        ],
        out_specs=pl.BlockSpec((batch, bn), lambda j: (0, j)),
        out_shape=jax.ShapeDtypeStruct((batch, n), F32),
        compiler_params=pltpu.CompilerParams(
            dimension_semantics=("arbitrary",), vmem_limit_bytes=VMEM_LIMIT_BYTES),
        name="ada",
    )(c, w_ada, b_ada.reshape(1, n))


def _in_proj_kernel(x_ref, g_ref, scale_ref, shift_ref, w_ref, z_ref):
    x = x_ref[...]
    ms = jnp.mean(x * x, axis=-1, keepdims=True)
    y = x * lax.rsqrt(ms + EPS) * g_ref[...]
    h = y * (1.0 + scale_ref[...]) + shift_ref[...]
    z_ref[...] = _dot(h.astype(BF16), w_ref[...])


def _in_proj(x, norm_g, scale, shift, w_in_bf16):
    batch, seq, d = x.shape
    n = w_in_bf16.shape[1]
    tm = IN_PROJ_ROWS
    return pl.pallas_call(
        _in_proj_kernel,
        grid=(batch, seq // tm),
        in_specs=[
            pl.BlockSpec((None, tm, d), lambda b, i: (b, i, 0)),
            pl.BlockSpec((1, d), lambda b, i: (0, 0)),
            pl.BlockSpec((None, 1, d), lambda b, i: (b, 0, 0)),
            pl.BlockSpec((None, 1, d), lambda b, i: (b, 0, 0)),
            pl.BlockSpec((d, n), lambda b, i: (0, 0)),
        ],
        out_specs=pl.BlockSpec((None, tm, n), lambda b, i: (b, i, 0)),
        out_shape=jax.ShapeDtypeStruct((batch, seq, n), F32),
        compiler_params=pltpu.CompilerParams(
            dimension_semantics=("arbitrary", "arbitrary"),
            vmem_limit_bytes=VMEM_LIMIT_BYTES),
        name="in_proj",
    )(x, norm_g.reshape(1, d), scale.reshape(batch, 1, d), shift.reshape(batch, 1, d),
      w_in_bf16)


LRU_PAD = SUBLANES
LRU_CHUNKS = 16
LRU_PITCH_PAD = SUBLANES


def _softplus(x):
    return jnp.maximum(x, 0.0) + jnp.log(1.0 + jnp.exp(-jnp.abs(x)))


def _rg_lru_kernel(xl_ref, gl_ref, cw_ref, cb_ref, wa_ref, ba_ref, wi_ref, bi_ref,
                   lam_ref, o_ref, xpad_ref, a_ref, b_ref, p_ref, h_ref):
    seq = xl_ref.shape[0]
    clen = seq // LRU_CHUNKS
    pitch = clen + LRU_PITCH_PAD
    groups = LRU_CHUNKS // SUBLANES

    xpad_ref[0:LRU_PAD, :] = jnp.zeros((LRU_PAD, LANES), F32)
    xpad_ref[LRU_PAD:LRU_PAD + seq, :] = xl_ref[...]
    xc = None
    for k in range(CONV_W):
        off = LRU_PAD - (CONV_W - 1) + k
        tap = cw_ref[k:k + 1, :] * xpad_ref[off:off + seq, :]
        xc = tap if xc is None else xc + tap
    xc = xc + cb_ref[...]
    xcb = xc.astype(BF16)
    t_r = jnp.tanh(_dot(xcb, wa_ref[...] * 0.5) + 0.5 * ba_ref[...])
    t_i = jnp.tanh(_dot(xcb, wi_ref[...] * 0.5) + 0.5 * bi_ref[...])
    half_decay = (-0.5 * LRU_C * LOG2E) * _softplus(-lam_ref[...])
    a = jnp.exp2(t_r * half_decay + half_decay)
    y = 1.0 - a * a
    mult = y * lax.rsqrt(jnp.maximum(y, TINY))
    first = lax.broadcasted_iota(jnp.int32, (SUBLANES, LANES), 0) == 0
    mult = jnp.concatenate([jnp.where(first, 1.0, mult[0:SUBLANES]), mult[SUBLANES:]], axis=0)
    bterm = (mult * xc) * (0.5 * t_i + 0.5)
    for c in range(LRU_CHUNKS):
        a_ref[c * pitch:c * pitch + clen, :] = a[c * clen:(c + 1) * clen]
        b_ref[c * pitch:c * pitch + clen, :] = bterm[c * clen:(c + 1) * clen]

    def body(s, carry):
        out = []
        for g in range(groups):
            h, p = carry[g]
            idx = pl.ds(s + g * SUBLANES * pitch, SUBLANES, stride=pitch)
            a_s = a_ref[idx, :]
            h = a_s * h + b_ref[idx, :]
            p = a_s * p
            h_ref[idx, :] = h
            p_ref[idx, :] = p
            out.append((h, p))
        return tuple(out)

    init = tuple((jnp.zeros((SUBLANES, LANES), F32), jnp.ones((SUBLANES, LANES), F32))
                 for _ in range(groups))
    ends = lax.fori_loop(0, clen, body, init, unroll=8)

    carry = jnp.zeros((1, LANES), F32)
    for c in range(LRU_CHUNKS):
        rows = slice(c * pitch, c * pitch + clen)
        h = h_ref[rows, :] + p_ref[rows, :] * carry
        out_rows = slice(c * clen, (c + 1) * clen)
        o_ref[out_rows, :] = (h * _silu(gl_ref[out_rows, :])).astype(o_ref.dtype)
        h_end, p_end = ends[c // SUBLANES]
        j = c % SUBLANES
        carry = h_end[j:j + 1, :] + p_end[j:j + 1, :] * carry


def _rg_lru(z, conv_w, conv_b, wa2, ba, wi2, bi, lam):
    batch, seq, _ = z.shape
    ngrp = D_LRU // LANES
    gl_off = D_LRU // LANES
    scan_rows = LRU_CHUNKS * (seq // LRU_CHUNKS + LRU_PITCH_PAD)
    row_spec = pl.BlockSpec((1, LANES), lambda b, j: (0, j))
    return pl.pallas_call(
        _rg_lru_kernel,
        grid=(batch, ngrp),
        in_specs=[
            pl.BlockSpec((None, seq, LANES), lambda b, j: (b, 0, j)),
            pl.BlockSpec((None, seq, LANES), lambda b, j: (b, 0, gl_off + j)),
            pl.BlockSpec((CONV_W, LANES), lambda b, j: (0, j)),
            row_spec,
            pl.BlockSpec((None, LANES, LANES), lambda b, j: (j, 0, 0)),
            row_spec,
            pl.BlockSpec((None, LANES, LANES), lambda b, j: (j, 0, 0)),
            row_spec,
            row_spec,
        ],
        out_specs=pl.BlockSpec((None, seq, LANES), lambda b, j: (b, 0, j)),
        out_shape=jax.ShapeDtypeStruct((batch, seq, D_LRU), BF16),
        scratch_shapes=[
            pltpu.VMEM((LRU_PAD + seq, LANES), F32),
            pltpu.VMEM((scan_rows, LANES), F32),
            pltpu.VMEM((scan_rows, LANES), F32),
            pltpu.VMEM((scan_rows, LANES), F32),
            pltpu.VMEM((scan_rows, LANES), F32),
        ],
        compiler_params=pltpu.CompilerParams(
            dimension_semantics=("arbitrary", "arbitrary"),
            vmem_limit_bytes=VMEM_LIMIT_BYTES),
        name="rg_lru",
    )(z, z, conv_w, conv_b.reshape(1, D_LRU), wa2, ba.reshape(1, D_LRU), wi2,
      bi.reshape(1, D_LRU), lam.reshape(1, D_LRU))


HEADS_PER_PAIR = LANES // HEAD_DIM
MOBA_PAIRS = 2
V_AUG_ROWS = HEAD_DIM + BF16_ROWS
SCORE_LOOKAHEAD = 2
SCORE_SLOTS = SCORE_LOOKAHEAD + 1


def _head_rms_scale(x, seg_bf16):
    ms = _dot((x * x).astype(BF16), seg_bf16)
    return x * lax.rsqrt(ms + EPS)


def _moba_kernel(q_ref, k_ref, v_ref, ga_ref, gq_ref, gk_ref, o_ref,
                 qs_ref, kb_ref, va_ref, bias_ref, s_ref, ot_ref):
    seq = q_ref.shape[0]
    npair = q_ref.shape[1] // LANES
    nb = seq // MOBA_BLOCK
    blk = MOBA_BLOCK
    first_ranked = (MOBA_TOPK + 1) * blk

    rr = lax.broadcasted_iota(jnp.int32, (LANES, LANES), 0) // HEAD_DIM
    cc = lax.broadcasted_iota(jnp.int32, (LANES, LANES), 1) // HEAD_DIM
    seg = jnp.where(rr == cc, 1.0 / HEAD_DIM, 0.0).astype(BF16)
    lane_head = lax.broadcasted_iota(jnp.int32, (1, LANES), 1) // HEAD_DIM
    lane_nb = lax.broadcasted_iota(jnp.int32, (nb, LANES), 1) // HEAD_DIM
    blk_row = lax.broadcasted_iota(jnp.int32, (nb, blk), 0)
    gqk = gq_ref[...] * gk_ref[...]

    def setup(c):
        cols = slice(c * LANES, (c + 1) * LANES)
        qn = _head_rms_scale(q_ref[:, cols], seg) * gqk
        kn = _head_rms_scale(k_ref[:, cols], seg)
        kb_ref[c] = kn.astype(BF16)
        vt = v_ref[:, cols].T
        for h in range(HEADS_PER_PAIR):
            q_scale = jnp.where(lane_head == h, HEAD_DIM ** -0.5 * LOG2E, 0.0)
            qs_ref[c, h] = (qn * q_scale).astype(BF16)
            va_ref[c, h, 0:HEAD_DIM, :] = vt[h * HEAD_DIM:(h + 1) * HEAD_DIM].astype(BF16)
            va_ref[c, h, HEAD_DIM:V_AUG_ROWS, :] = jnp.ones((BF16_ROWS, seq), BF16)

        kmean = jnp.concatenate(
            [jnp.mean(kn[n * blk:(n + 1) * blk, :], axis=0, keepdims=True) for n in range(nb)],
            axis=0)
        gmat = jnp.concatenate(
            [jnp.where(lane_nb == h, kmean, 0.0) for h in range(HEADS_PER_PAIR)], axis=0)
        g_hi, g_lo = _split_bf16(gmat)
        q_hi, q_lo = _split_bf16(qn[first_ranked:])
        gate = _dot_nt(g_hi, q_hi) + _dot_nt(g_lo, q_hi) + _dot_nt(g_hi, q_lo)

        for h in range(HEADS_PER_PAIR):
            for qb in range(nb):
                past = blk_row < qb
                if qb > MOBA_TOPK:
                    col = qb * blk - first_ranked
                    g = gate[h * nb:(h + 1) * nb, col:col + blk]
                    cnt = jnp.zeros((nb, blk), jnp.int32)
                    for m in range(qb):
                        gm = g[m:m + 1, :]
                        beats = (gm > g) | ((gm == g) & (m < blk_row))
                        cnt = cnt + beats.astype(jnp.int32)
                    chosen = past & (cnt < MOBA_TOPK)
                else:
                    chosen = past
                bias_ref[c, h, :, qb * blk:(qb + 1) * blk] = jnp.where(chosen, 0.0, MASKED)

    k_pos = lax.broadcasted_iota(jnp.int32, (blk, blk), 0)
    q_pos = lax.broadcasted_iota(jnp.int32, (blk, blk), 1)
    causal = k_pos <= q_pos

    def scores(slot, c, h, qb):
        q_blk = qs_ref[c, h, qb * blk:(qb + 1) * blk, :]
        m_run = None
        biases = []
        for kb in range(qb + 1):
            s = _dot_nt(kb_ref[c, kb * blk:(kb + 1) * blk, :], q_blk)
            if kb == qb:
                s = jnp.where(causal, s, MASKED)
            s_ref[slot, kb * blk:(kb + 1) * blk, :] = s
            m_blk = jnp.max(s, axis=0, keepdims=True)
            if kb == qb:
                biases.append(None)
            else:
                b_row = bias_ref[c, h, kb:kb + 1, qb * blk:(qb + 1) * blk]
                m_blk = m_blk + b_row
                biases.append(b_row)
            m_run = m_blk if m_run is None else jnp.maximum(m_run, m_blk)
        return [m_run if b is None else m_run - b for b in biases]

    def attend(slot, c, h, qb, subs):
        ps = []
        for kb in range(qb + 1):
            p = jnp.exp2(s_ref[slot, kb * blk:(kb + 1) * blk, :] - subs[kb])
            ps.append(p.astype(BF16))
        p_all = ps[0] if len(ps) == 1 else jnp.concatenate(ps, axis=0)
        n_keys = (qb + 1) * blk
        o_aug = _dot(va_ref[c, h, :, 0:n_keys], p_all)
        inv_l = 1.0 / o_aug[HEAD_DIM:HEAD_DIM + 1, :]
        ot_ref[c, h * HEAD_DIM:(h + 1) * HEAD_DIM, qb * blk:(qb + 1) * blk] = (
            o_aug[0:HEAD_DIM, :] * inv_l)

    for c in range(npair):
        setup(c)

    items = [(c, h, qb) for c in range(npair) for h in range(HEADS_PER_PAIR)
             for qb in range(nb)]
    pending = {}
    for i in range(min(SCORE_LOOKAHEAD, len(items))):
        pending[i] = scores(i % SCORE_SLOTS, *items[i])
    for i, item in enumerate(items):
        j = i + SCORE_LOOKAHEAD
        if j < len(items):
            pending[j] = scores(j % SCORE_SLOTS, *items[j])
        attend(i % SCORE_SLOTS, *item, pending.pop(i))

    for c in range(npair):
        cols = slice(c * LANES, (c + 1) * LANES)
        o_ref[:, cols] = (ot_ref[c].T * _silu(ga_ref[:, cols])).astype(o_ref.dtype)


def _moba(z, gq2, gk2):
    batch, seq, _ = z.shape
    width = MOBA_PAIRS * LANES
    ncell = D_ATTN // width
    nb = seq // MOBA_BLOCK
    q_off = 2 * D_LRU // width
    k_off = q_off + ncell
    v_off = k_off + ncell
    ga_off = v_off + ncell

    def col_spec(off):
        return pl.BlockSpec((None, seq, width), lambda b, j: (b, 0, off + j))

    gain_spec = pl.BlockSpec((1, LANES), lambda b, j: (0, 0))
    return pl.pallas_call(
        _moba_kernel,
        grid=(batch, ncell),
        in_specs=[col_spec(q_off), col_spec(k_off), col_spec(v_off), col_spec(ga_off),
                  gain_spec, gain_spec],
        out_specs=pl.BlockSpec((None, seq, width), lambda b, j: (b, 0, j)),
        out_shape=jax.ShapeDtypeStruct((batch, seq, D_ATTN), BF16),
        scratch_shapes=[
            pltpu.VMEM((MOBA_PAIRS, HEADS_PER_PAIR, seq, LANES), BF16),
            pltpu.VMEM((MOBA_PAIRS, seq, LANES), BF16),
            pltpu.VMEM((MOBA_PAIRS, HEADS_PER_PAIR, V_AUG_ROWS, seq), BF16),
            pltpu.VMEM((MOBA_PAIRS, HEADS_PER_PAIR, nb, seq), F32),
            pltpu.VMEM((SCORE_SLOTS, seq, MOBA_BLOCK), F32),
            pltpu.VMEM((MOBA_PAIRS, LANES, seq), F32),
        ],
        compiler_params=pltpu.CompilerParams(
            dimension_semantics=("arbitrary", "arbitrary"),
            vmem_limit_bytes=VMEM_LIMIT_BYTES),
        name="moba",
    )(z, z, z, z, gq2, gk2)


def _out_proj_kernel(yl_ref, ya_ref, w_ref, x_ref, gate_ref, o_ref):
    y = _dot(yl_ref[...], w_ref[0:D_LRU, :]) + _dot(ya_ref[...], w_ref[D_LRU:, :])
    o_ref[...] = x_ref[...] + gate_ref[...] * y


def _out_proj(y_lru, y_att, w_out_bf16, x, gate):
    batch, seq, d = x.shape
    tm = OUT_PROJ_ROWS
    return pl.pallas_call(
        _out_proj_kernel,
        grid=(batch, seq // tm),
        in_specs=[
            pl.BlockSpec((None, tm, D_LRU), lambda b, i: (b, i, 0)),
            pl.BlockSpec((None, tm, D_ATTN), lambda b, i: (b, i, 0)),
            pl.BlockSpec((D_LRU + D_ATTN, d), lambda b, i: (0, 0)),
            pl.BlockSpec((None, tm, d), lambda b, i: (b, i, 0)),
            pl.BlockSpec((None, 1, d), lambda b, i: (b, 0, 0)),
        ],
        out_specs=pl.BlockSpec((None, tm, d), lambda b, i: (b, i, 0)),
        out_shape=jax.ShapeDtypeStruct((batch, seq, d), F32),
        compiler_params=pltpu.CompilerParams(
            dimension_semantics=("arbitrary", "arbitrary"),
            vmem_limit_bytes=VMEM_LIMIT_BYTES),
        name="out_proj",
    )(y_lru, y_att, w_out_bf16, x, gate.reshape(batch, 1, d))


def _block_diag_pairs(w):
    nblk, bw, _ = w.shape
    w = w.reshape(nblk // 2, 2, bw, bw)
    zero = jnp.zeros_like(w[:, 0])
    top = jnp.concatenate([w[:, 0], zero], axis=-1)
    bot = jnp.concatenate([zero, w[:, 1]], axis=-1)
    return jnp.concatenate([top, bot], axis=-2)


def _layer(x, c, w_ada, b_ada, norm_g, w_in, conv_w, conv_b, lru_wa, lru_ba, lru_wi,
           lru_bi, lru_lambda, q_norm_g, k_norm_g, w_out):
    d = x.shape[-1]
    ada = _ada(c, w_ada, b_ada)
    shift, scale, gate = ada[:, :d], ada[:, d:2 * d], ada[:, 2 * d:]
    z = _in_proj(x, norm_g, scale, shift, w_in.astype(BF16))
    y_lru = _rg_lru(z, conv_w, conv_b, _block_diag_pairs(lru_wa).astype(BF16), lru_ba,
                    _block_diag_pairs(lru_wi).astype(BF16), lru_bi, lru_lambda)
    gq2 = jnp.tile(q_norm_g, HEADS_PER_PAIR).reshape(1, LANES)
    gk2 = jnp.tile(k_norm_g, HEADS_PER_PAIR).reshape(1, LANES)
    y_att = _moba(z, gq2, gk2)
    return _out_proj(y_lru, y_att, w_out.astype(BF16), x, gate)


def kernel(x, c, w_ada, b_ada, norm_g, w_in, conv_w, conv_b, lru_wa, lru_ba, lru_wi,
           lru_bi, lru_lambda, q_norm_g, k_norm_g, w_out):
    depth = w_ada.shape[0]
    for l in range(depth):
        x = _layer(x, c, w_ada[l], b_ada[l], norm_g[l], w_in[l], conv_w[l], conv_b[l],
                   lru_wa[l], lru_ba[l], lru_wi[l], lru_bi[l], lru_lambda[l],
                   q_norm_g[l], k_norm_g[l], w_out[l])
    return x
```

```python
import math

import jax
import jax.numpy as jnp
from jax import lax
from jax.experimental import pallas as pl
from jax.experimental.pallas import tpu as pltpu

F32 = jnp.float32
BF16 = jnp.bfloat16

D_MODEL = 1024
D_LRU = 512
D_ATTN = 512
LRU_BLOCK_W = 64
CONV_W = 4
LRU_C = 8.0
HEAD_DIM = 64
MOBA_BLOCK = 256
MOBA_TOPK = 3
EPS = 1e-6
D_IN = 2 * D_LRU + 4 * D_ATTN

SUBLANES = 8
LANES = 128
BF16_ROWS = 16
VMEM_LIMIT_BYTES = 56 * 1024 * 1024
IN_PROJ_ROWS = 1024
OUT_PROJ_ROWS = 1024

LOG2E = math.log2(math.e)
MASKED = -1e30
TINY = 1e-37


def _dot(a, b):
    return jnp.dot(a, b, preferred_element_type=F32)


def _dot_nt(a, b):
    return lax.dot_general(a, b, (((1,), (1,)), ((), ())), preferred_element_type=F32)


def _split_bf16(x):
    hi = x.astype(BF16)
    lo = (x - hi.astype(F32)).astype(BF16)
    return hi, lo


def _silu(x):
    hx = 0.5 * x
    return hx * (1.0 + jnp.tanh(hx))


def _ada_kernel(c_ref, w_ref, b_ref, o_ref):
    sc = _silu(c_ref[...])
    s_hi, s_lo = _split_bf16(sc)
    w = w_ref[...]
    w_hi, w_lo = _split_bf16(w)
    acc = _dot(s_hi, w_hi) + _dot(s_lo, w_hi) + _dot(s_hi, w_lo)
    o_ref[...] = acc + b_ref[...]


def _ada(c, w_ada, b_ada):
    batch, d = c.shape
    n = w_ada.shape[1]
    bn = 768
    return pl.pallas_call(
        _ada_kernel,
        grid=(n // bn,),
        in_specs=[
            pl.BlockSpec((batch, d), lambda j: (0, 0)),
            pl.BlockSpec((d, bn), lambda j: (0, j)),
            pl.BlockSpec((1, bn), lambda j: (0, j)),
        ],
        out_specs=pl.BlockSpec((batch, bn), lambda j: (0, j)),
        out_shape=jax.ShapeDtypeStruct((batch, n), F32),
        compiler_params=pltpu.CompilerParams(
            dimension_semantics=("arbitrary",), vmem_limit_bytes=VMEM_LIMIT_BYTES),
        name="ada",
    )(c, w_ada, b_ada.reshape(1, n))


def _in_proj_kernel(x_ref, g_ref, scale_ref, shift_ref, w_ref, z_ref):
    x = x_ref[...]
    ms = jnp.mean(x * x, axis=-1, keepdims=True)
    y = x * lax.rsqrt(ms + EPS) * g_ref[...]
    h = y * (1.0 + scale_ref[...]) + shift_ref[...]
    z_ref[...] = _dot(h.astype(BF16), w_ref[...])


def _in_proj(x, norm_g, scale, shift, w_in_bf16):
    batch, seq, d = x.shape
    n = w_in_bf16.shape[1]
    tm = IN_PROJ_ROWS
    return pl.pallas_call(
        _in_proj_kernel,
        grid=(batch, seq // tm),
        in_specs=[
            pl.BlockSpec((None, tm, d), lambda b, i: (b, i, 0)),
            pl.BlockSpec((1, d), lambda b, i: (0, 0)),
            pl.BlockSpec((None, 1, d), lambda b, i: (b, 0, 0)),
            pl.BlockSpec((None, 1, d), lambda b, i: (b, 0, 0)),
            pl.BlockSpec((d, n), lambda b, i: (0, 0)),
        ],
        out_specs=pl.BlockSpec((None, tm, n), lambda b, i: (b, i, 0)),
        out_shape=jax.ShapeDtypeStruct((batch, seq, n), F32),
        compiler_params=pltpu.CompilerParams(
            dimension_semantics=("arbitrary", "arbitrary"),
            vmem_limit_bytes=VMEM_LIMIT_BYTES),
        name="in_proj",
    )(x, norm_g.reshape(1, d), scale.reshape(batch, 1, d), shift.reshape(batch, 1, d),
      w_in_bf16)


LRU_PAD = SUBLANES
LRU_CHUNKS = 16
LRU_PITCH_PAD = SUBLANES


def _softplus(x):
    return jnp.maximum(x, 0.0) + jnp.log(1.0 + jnp.exp(-jnp.abs(x)))


def _rg_lru_kernel(xl_ref, gl_ref, cw_ref, cb_ref, wa_ref, ba_ref, wi_ref, bi_ref,
                   lam_ref, o_ref, xpad_ref, a_ref, b_ref, p_ref, h_ref):
    seq = xl_ref.shape[0]
    clen = seq // LRU_CHUNKS
    pitch = clen + LRU_PITCH_PAD
    groups = LRU_CHUNKS // SUBLANES

    xpad_ref[0:LRU_PAD, :] = jnp.zeros((LRU_PAD, LANES), F32)
    xpad_ref[LRU_PAD:LRU_PAD + seq, :] = xl_ref[...]
    xc = None
    for k in range(CONV_W):
        off = LRU_PAD - (CONV_W - 1) + k
        tap = cw_ref[k:k + 1, :] * xpad_ref[off:off + seq, :]
        xc = tap if xc is None else xc + tap
    xc = xc + cb_ref[...]
    xcb = xc.astype(BF16)
    t_r = jnp.tanh(_dot(xcb, wa_ref[...] * 0.5) + 0.5 * ba_ref[...])
    t_i = jnp.tanh(_dot(xcb, wi_ref[...] * 0.5) + 0.5 * bi_ref[...])
    half_decay = (-0.5 * LRU_C * LOG2E) * _softplus(-lam_ref[...])
    a = jnp.exp2(t_r * half_decay + half_decay)
    y = 1.0 - a * a
    mult = y * lax.rsqrt(jnp.maximum(y, TINY))
    first = lax.broadcasted_iota(jnp.int32, (SUBLANES, LANES), 0) == 0
    mult = jnp.concatenate([jnp.where(first, 1.0, mult[0:SUBLANES]), mult[SUBLANES:]], axis=0)
    bterm = (mult * xc) * (0.5 * t_i + 0.5)
    for c in range(LRU_CHUNKS):
        a_ref[c * pitch:c * pitch + clen, :] = a[c * clen:(c + 1) * clen]
        b_ref[c * pitch:c * pitch + clen, :] = bterm[c * clen:(c + 1) * clen]

    def body(s, carry):
        out = []
        for g in range(groups):
            h, p = carry[g]
            idx = pl.ds(s + g * SUBLANES * pitch, SUBLANES, stride=pitch)
            a_s = a_ref[idx, :]
            h = a_s * h + b_ref[idx, :]
            p = a_s * p
            h_ref[idx, :] = h
            p_ref[idx, :] = p
            out.append((h, p))
        return tuple(out)

    init = tuple((jnp.zeros((SUBLANES, LANES), F32), jnp.ones((SUBLANES, LANES), F32))
                 for _ in range(groups))
    ends = lax.fori_loop(0, clen, body, init, unroll=8)

    carry = jnp.zeros((1, LANES), F32)
    for c in range(LRU_CHUNKS):
        rows = slice(c * pitch, c * pitch + clen)
        h = h_ref[rows, :] + p_ref[rows, :] * carry
        out_rows = slice(c * clen, (c + 1) * clen)
        o_ref[out_rows, :] = (h * _silu(gl_ref[out_rows, :])).astype(o_ref.dtype)
        h_end, p_end = ends[c // SUBLANES]
        j = c % SUBLANES
        carry = h_end[j:j + 1, :] + p_end[j:j + 1, :] * carry


def _rg_lru(z, conv_w, conv_b, wa2, ba, wi2, bi, lam):
    batch, seq, _ = z.shape
    ngrp = D_LRU // LANES
    gl_off = D_LRU // LANES
    scan_rows = LRU_CHUNKS * (seq // LRU_CHUNKS + LRU_PITCH_PAD)
    row_spec = pl.BlockSpec((1, LANES), lambda b, j: (0, j))
    return pl.pallas_call(
        _rg_lru_kernel,
        grid=(batch, ngrp),
        in_specs=[
            pl.BlockSpec((None, seq, LANES), lambda b, j: (b, 0, j)),
            pl.BlockSpec((None, seq, LANES), lambda b, j: (b, 0, gl_off + j)),
            pl.BlockSpec((CONV_W, LANES), lambda b, j: (0, j)),
            row_spec,
            pl.BlockSpec((None, LANES, LANES), lambda b, j: (j, 0, 0)),
            row_spec,
            pl.BlockSpec((None, LANES, LANES), lambda b, j: (j, 0, 0)),
            row_spec,
            row_spec,
        ],
        out_specs=pl.BlockSpec((None, seq, LANES), lambda b, j: (b, 0, j)),
        out_shape=jax.ShapeDtypeStruct((batch, seq, D_LRU), BF16),
        scratch_shapes=[
            pltpu.VMEM((LRU_PAD + seq, LANES), F32),
            pltpu.VMEM((scan_rows, LANES), F32),
            pltpu.VMEM((scan_rows, LANES), F32),
            pltpu.VMEM((scan_rows, LANES), F32),
            pltpu.VMEM((scan_rows, LANES), F32),
        ],
        compiler_params=pltpu.CompilerParams(
            dimension_semantics=("arbitrary", "arbitrary"),
            vmem_limit_bytes=VMEM_LIMIT_BYTES),
        name="rg_lru",
    )(z, z, conv_w, conv_b.reshape(1, D_LRU), wa2, ba.reshape(1, D_LRU), wi2,
      bi.reshape(1, D_LRU), lam.reshape(1, D_LRU))


HEADS_PER_PAIR = LANES // HEAD_DIM
MOBA_PAIRS = 2
V_AUG_ROWS = HEAD_DIM + BF16_ROWS
SCORE_LOOKAHEAD = 2
SCORE_SLOTS = SCORE_LOOKAHEAD + 1


def _head_rms_scale(x, seg_bf16):
    ms = _dot((x * x).astype(BF16), seg_bf16)
    return x * lax.rsqrt(ms + EPS)


def _moba_kernel(q_ref, k_ref, v_ref, ga_ref, gq_ref, gk_ref, o_ref,
                 qs_ref, kb_ref, va_ref, bias_ref, s_ref, ot_ref):
    seq = q_ref.shape[0]
    npair = q_ref.shape[1] // LANES
    nb = seq // MOBA_BLOCK
    blk = MOBA_BLOCK
    first_ranked = (MOBA_TOPK + 1) * blk

    rr = lax.broadcasted_iota(jnp.int32, (LANES, LANES), 0) // HEAD_DIM
    cc = lax.broadcasted_iota(jnp.int32, (LANES, LANES), 1) // HEAD_DIM
    seg = jnp.where(rr == cc, 1.0 / HEAD_DIM, 0.0).astype(BF16)
    lane_head = lax.broadcasted_iota(jnp.int32, (1, LANES), 1) // HEAD_DIM
    lane_nb = lax.broadcasted_iota(jnp.int32, (nb, LANES), 1) // HEAD_DIM
    blk_row = lax.broadcasted_iota(jnp.int32, (nb, blk), 0)
    gqk = gq_ref[...] * gk_ref[...]

    def setup(c):
        cols = slice(c * LANES, (c + 1) * LANES)
        qn = _head_rms_scale(q_ref[:, cols], seg) * gqk
        kn = _head_rms_scale(k_ref[:, cols], seg)
        kb_ref[c] = kn.astype(BF16)
        vt = v_ref[:, cols].T
        for h in range(HEADS_PER_PAIR):
            q_scale = jnp.where(lane_head == h, HEAD_DIM ** -0.5 * LOG2E, 0.0)
            qs_ref[c, h] = (qn * q_scale).astype(BF16)
            va_ref[c, h, 0:HEAD_DIM, :] = vt[h * HEAD_DIM:(h + 1) * HEAD_DIM].astype(BF16)
            va_ref[c, h, HEAD_DIM:V_AUG_ROWS, :] = jnp.ones((BF16_ROWS, seq), BF16)

        kmean = jnp.concatenate(
            [jnp.mean(kn[n * blk:(n + 1) * blk, :], axis=0, keepdims=True) for n in range(nb)],
            axis=0)
        gmat = jnp.concatenate(
            [jnp.where(lane_nb == h, kmean, 0.0) for h in range(HEADS_PER_PAIR)], axis=0)
        g_hi, g_lo = _split_bf16(gmat)
        q_hi, q_lo = _split_bf16(qn[first_ranked:])
        gate = _dot_nt(g_hi, q_hi) + _dot_nt(g_lo, q_hi) + _dot_nt(g_hi, q_lo)

        for h in range(HEADS_PER_PAIR):
            for qb in range(nb):
                past = blk_row < qb
                if qb > MOBA_TOPK:
                    col = qb * blk - first_ranked
                    g = gate[h * nb:(h + 1) * nb, col:col + blk]
                    cnt = jnp.zeros((nb, blk), jnp.int32)
                    for m in range(qb):
                        gm = g[m:m + 1, :]
                        beats = (gm > g) | ((gm == g) & (m < blk_row))
                        cnt = cnt + beats.astype(jnp.int32)
                    chosen = past & (cnt < MOBA_TOPK)
                else:
                    chosen = past
                bias_ref[c, h, :, qb * blk:(qb + 1) * blk] = jnp.where(chosen, 0.0, MASKED)

    k_pos = lax.broadcasted_iota(jnp.int32, (blk, blk), 0)
    q_pos = lax.broadcasted_iota(jnp.int32, (blk, blk), 1)
    causal = k_pos <= q_pos

    def scores(slot, c, h, qb):
        q_blk = qs_ref[c, h, qb * blk:(qb + 1) * blk, :]
        m_run = None
        biases = []
        for kb in range(qb + 1):
            s = _dot_nt(kb_ref[c, kb * blk:(kb + 1) * blk, :], q_blk)
            if kb == qb:
                s = jnp.where(causal, s, MASKED)
            s = s.astype(BF16)
            s_ref[slot, kb * blk:(kb + 1) * blk, :] = s
            m_blk = jnp.max(s, axis=0, keepdims=True).astype(F32)
            if kb == qb:
                biases.append(None)
            else:
                b_row = bias_ref[c, h, kb:kb + 1, qb * blk:(qb + 1) * blk]
                m_blk = m_blk + b_row
                biases.append(b_row)
            m_run = m_blk if m_run is None else jnp.maximum(m_run, m_blk)
        return [m_run if b is None else m_run - b for b in biases]

    def attend(slot, c, h, qb, subs):
        ps = []
        for kb in range(qb + 1):
            ps.append(jnp.exp2(s_ref[slot, kb * blk:(kb + 1) * blk, :] - subs[kb].astype(BF16)))
        p_all = ps[0] if len(ps) == 1 else jnp.concatenate(ps, axis=0)
        n_keys = (qb + 1) * blk
        o_aug = _dot(va_ref[c, h, :, 0:n_keys], p_all)
        inv_l = 1.0 / o_aug[HEAD_DIM:HEAD_DIM + 1, :]
        ot_ref[c, h * HEAD_DIM:(h + 1) * HEAD_DIM, qb * blk:(qb + 1) * blk] = (
            o_aug[0:HEAD_DIM, :] * inv_l)

    for c in range(npair):
        setup(c)

    items = [(c, h, qb) for c in range(npair) for h in range(HEADS_PER_PAIR)
             for qb in range(nb)]
    pending = {}
    for i in range(min(SCORE_LOOKAHEAD, len(items))):
        pending[i] = scores(i % SCORE_SLOTS, *items[i])
    for i, item in enumerate(items):
        j = i + SCORE_LOOKAHEAD
        if j < len(items):
            pending[j] = scores(j % SCORE_SLOTS, *items[j])
        attend(i % SCORE_SLOTS, *item, pending.pop(i))

    for c in range(npair):
        cols = slice(c * LANES, (c + 1) * LANES)
        o_ref[:, cols] = (ot_ref[c].T * _silu(ga_ref[:, cols])).astype(o_ref.dtype)


def _moba(z, gq2, gk2):
    batch, seq, _ = z.shape
    width = MOBA_PAIRS * LANES
    ncell = D_ATTN // width
    nb = seq // MOBA_BLOCK
    q_off = 2 * D_LRU // width
    k_off = q_off + ncell
    v_off = k_off + ncell
    ga_off = v_off + ncell

    def col_spec(off):
        return pl.BlockSpec((None, seq, width), lambda b, j: (b, 0, off + j))

    gain_spec = pl.BlockSpec((1, LANES), lambda b, j: (0, 0))
    return pl.pallas_call(
        _moba_kernel,
        grid=(batch, ncell),
        in_specs=[col_spec(q_off), col_spec(k_off), col_spec(v_off), col_spec(ga_off),
                  gain_spec, gain_spec],
        out_specs=pl.BlockSpec((None, seq, width), lambda b, j: (b, 0, j)),
        out_shape=jax.ShapeDtypeStruct((batch, seq, D_ATTN), BF16),
        scratch_shapes=[
            pltpu.VMEM((MOBA_PAIRS, HEADS_PER_PAIR, seq, LANES), BF16),
            pltpu.VMEM((MOBA_PAIRS, seq, LANES), BF16),
            pltpu.VMEM((MOBA_PAIRS, HEADS_PER_PAIR, V_AUG_ROWS, seq), BF16),
            pltpu.VMEM((MOBA_PAIRS, HEADS_PER_PAIR, nb, seq), F32),
            pltpu.VMEM((SCORE_SLOTS, seq, MOBA_BLOCK), BF16),
            pltpu.VMEM((MOBA_PAIRS, LANES, seq), F32),
        ],
        compiler_params=pltpu.CompilerParams(
            dimension_semantics=("arbitrary", "arbitrary"),
            vmem_limit_bytes=VMEM_LIMIT_BYTES),
        name="moba",
    )(z, z, z, z, gq2, gk2)


def _out_proj_kernel(yl_ref, ya_ref, w_ref, x_ref, gate_ref, o_ref):
    y = _dot(yl_ref[...], w_ref[0:D_LRU, :]) + _dot(ya_ref[...], w_ref[D_LRU:, :])
    o_ref[...] = x_ref[...] + gate_ref[...] * y


def _out_proj(y_lru, y_att, w_out_bf16, x, gate):
    batch, seq, d = x.shape
    tm = OUT_PROJ_ROWS
    return pl.pallas_call(
        _out_proj_kernel,
        grid=(batch, seq // tm),
        in_specs=[
            pl.BlockSpec((None, tm, D_LRU), lambda b, i: (b, i, 0)),
            pl.BlockSpec((None, tm, D_ATTN), lambda b, i: (b, i, 0)),
            pl.BlockSpec((D_LRU + D_ATTN, d), lambda b, i: (0, 0)),
            pl.BlockSpec((None, tm, d), lambda b, i: (b, i, 0)),
            pl.BlockSpec((None, 1, d), lambda b, i: (b, 0, 0)),
        ],
        out_specs=pl.BlockSpec((None, tm, d), lambda b, i: (b, i, 0)),
        out_shape=jax.ShapeDtypeStruct((batch, seq, d), F32),
        compiler_params=pltpu.CompilerParams(
            dimension_semantics=("arbitrary", "arbitrary"),
            vmem_limit_bytes=VMEM_LIMIT_BYTES),
        name="out_proj",
    )(y_lru, y_att, w_out_bf16, x, gate.reshape(batch, 1, d))


def _block_diag_pairs(w):
    nblk, bw, _ = w.shape
    w = w.reshape(nblk // 2, 2, bw, bw)
    zero = jnp.zeros_like(w[:, 0])
    top = jnp.concatenate([w[:, 0], zero], axis=-1)
    bot = jnp.concatenate([zero, w[:, 1]], axis=-1)
    return jnp.concatenate([top, bot], axis=-2)


def _layer(x, c, w_ada, b_ada, norm_g, w_in, conv_w, conv_b, lru_wa, lru_ba, lru_wi,
           lru_bi, lru_lambda, q_norm_g, k_norm_g, w_out):
    d = x.shape[-1]
    ada = _ada(c, w_ada, b_ada)
    shift, scale, gate = ada[:, :d], ada[:, d:2 * d], ada[:, 2 * d:]
    z = _in_proj(x, norm_g, scale, shift, w_in.astype(BF16))
    y_lru = _rg_lru(z, conv_w, conv_b, _block_diag_pairs(lru_wa).astype(BF16), lru_ba,
                    _block_diag_pairs(lru_wi).astype(BF16), lru_bi, lru_lambda)
    gq2 = jnp.tile(q_norm_g, HEADS_PER_PAIR).reshape(1, LANES)
    gk2 = jnp.tile(k_norm_g, HEADS_PER_PAIR).reshape(1, LANES)
    y_att = _moba(z, gq2, gk2)
    return _out_proj(y_lru, y_att, w_out.astype(BF16), x, gate)


def kernel(x, c, w_ada, b_ada, norm_g, w_in, conv_w, conv_b, lru_wa, lru_ba, lru_wi,
           lru_bi, lru_lambda, q_norm_g, k_norm_g, w_out):
    depth = w_ada.shape[0]
    for l in range(depth):
        x = _layer(x, c, w_ada[l], b_ada[l], norm_g[l], w_in[l], conv_w[l], conv_b[l],
                   lru_wa[l], lru_ba[l], lru_wi[l], lru_bi[l], lru_lambda[l],
                   q_norm_g[l], k_norm_g[l], w_out[l])
    return x
```

```python
import math

import jax
import jax.numpy as jnp
from jax import lax
from jax.experimental import pallas as pl
from jax.experimental.pallas import tpu as pltpu

F32 = jnp.float32
BF16 = jnp.bfloat16

D_MODEL = 1024
D_LRU = 512
D_ATTN = 512
LRU_BLOCK_W = 64
CONV_W = 4
LRU_C = 8.0
HEAD_DIM = 64
MOBA_BLOCK = 256
MOBA_TOPK = 3
EPS = 1e-6
D_IN = 2 * D_LRU + 4 * D_ATTN

SUBLANES = 8
LANES = 128
BF16_ROWS = 16
VMEM_LIMIT_BYTES = 56 * 1024 * 1024
IN_LRU_ROWS = 512
OUT_PROJ_ROWS = 2048

LOG2E = math.log2(math.e)
MASKED = -1e30
TINY = 1e-37


def _dot(a, b):
    return jnp.dot(a, b, preferred_element_type=F32)


def _dot_nt(a, b):
    return lax.dot_general(a, b, (((1,), (1,)), ((), ())), preferred_element_type=F32)


def _split_bf16(x):
    hi = x.astype(BF16)
    lo = (x - hi.astype(F32)).astype(BF16)
    return hi, lo


def _silu(x):
    hx = 0.5 * x
    return hx * (1.0 + jnp.tanh(hx))


def _ada_kernel(c_ref, w_ref, b_ref, o_ref):
    sc = _silu(c_ref[...])
    s_hi, s_lo = _split_bf16(sc)
    w = w_ref[...]
    w_hi, w_lo = _split_bf16(w)
    acc = _dot(s_hi, w_hi) + _dot(s_lo, w_hi) + _dot(s_hi, w_lo)
    o_ref[...] = acc + b_ref[...]


def _ada(c, w_ada, b_ada):
    batch, d = c.shape
    n = w_ada.shape[1]
    bn = 768
    return pl.pallas_call(
        _ada_kernel,
        grid=(n // bn,),
        in_specs=[
            pl.BlockSpec((batch, d), lambda j: (0, 0)),
            pl.BlockSpec((d, bn), lambda j: (0, j)),
            pl.BlockSpec((1, bn), lambda j: (0, j)),
        ],
        out_specs=pl.BlockSpec((batch, bn), lambda j: (0, j)),
        out_shape=jax.ShapeDtypeStruct((batch, n), F32),
        compiler_params=pltpu.CompilerParams(
            dimension_semantics=("arbitrary",), vmem_limit_bytes=VMEM_LIMIT_BYTES),
        name="ada",
    )(c, w_ada, b_ada.reshape(1, n))


LRU_PAD = SUBLANES
LRU_CHUNKS = 16
LRU_GROUPS = D_LRU // LANES
LRU_PITCH_PAD = SUBLANES
ATT_CHUNK = 256


def _softplus(x):
    return jnp.maximum(x, 0.0) + jnp.log(1.0 + jnp.exp(-jnp.abs(x)))


def _in_lru_kernel(x_ref, g_ref, scale_ref, shift_ref, w_ref, cw_ref, cb_ref, wa_ref, ba_ref,
                   wi_ref, bi_ref, lam_ref, z_ref, y_ref,
                   zl_ref, xpad_ref, a_ref, b_ref, p_ref, h_ref, state_ref):
    tm = x_ref.shape[0]
    clen = tm // LRU_CHUNKS
    pitch = clen + LRU_PITCH_PAD
    chunk_groups = LRU_CHUNKS // SUBLANES
    first_tile = pl.program_id(1) == 0

    x = x_ref[...]
    ms = jnp.mean(x * x, axis=-1, keepdims=True)
    xn = x * lax.rsqrt(ms + EPS) * g_ref[...]
    hb = (xn * (1.0 + scale_ref[...]) + shift_ref[...]).astype(BF16)
    zl_ref[...] = _dot(hb, w_ref[:, 0:2 * D_LRU])

    n_att_chunks = z_ref.shape[1] // ATT_CHUNK
    att_chunks_done = [0]

    def att_chunks(count):
        for _ in range(count):
            i = att_chunks_done[0]
            if i < n_att_chunks:
                cols = slice(i * ATT_CHUNK, (i + 1) * ATT_CHUNK)
                z_ref[:, cols] = _dot(
                    hb, w_ref[:, 2 * D_LRU + i * ATT_CHUNK:2 * D_LRU + (i + 1) * ATT_CHUNK])
                att_chunks_done[0] = i + 1

    @pl.when(first_tile)
    def _():
        xpad_ref[0:LRU_PAD, :] = jnp.zeros((LRU_PAD, D_LRU), F32)
        state_ref[...] = jnp.zeros_like(state_ref)

    xpad_ref[LRU_PAD:LRU_PAD + tm, :] = zl_ref[:, 0:D_LRU]
    first_row = lax.broadcasted_iota(jnp.int32, (SUBLANES, LANES), 0) == 0
    starts_sequence = jnp.logical_and(first_row, first_tile)
    for g in range(LRU_GROUPS):
        att_chunks(1)
        cols = slice(g * LANES, (g + 1) * LANES)
        xc = None
        for k in range(CONV_W):
            off = LRU_PAD - (CONV_W - 1) + k
            tap = cw_ref[k:k + 1, cols] * xpad_ref[off:off + tm, cols]
            xc = tap if xc is None else xc + tap
        xc = xc + cb_ref[:, cols]
        xcb = xc.astype(BF16)
        t_r = jnp.tanh(_dot(xcb, wa_ref[g] * 0.5) + 0.5 * ba_ref[:, cols])
        t_i = jnp.tanh(_dot(xcb, wi_ref[g] * 0.5) + 0.5 * bi_ref[:, cols])
        half_decay = (-0.5 * LRU_C * LOG2E) * _softplus(-lam_ref[:, cols])
        a = jnp.exp2(t_r * half_decay + half_decay)
        y = 1.0 - a * a
        mult = y * lax.rsqrt(jnp.maximum(y, TINY))
        mult = jnp.concatenate(
            [jnp.where(starts_sequence, 1.0, mult[0:SUBLANES]), mult[SUBLANES:]], axis=0)
        bterm = (mult * xc) * (0.5 * t_i + 0.5)
        for c in range(LRU_CHUNKS):
            a_ref[g, c * pitch:c * pitch + clen, :] = a[c * clen:(c + 1) * clen]
            b_ref[g, c * pitch:c * pitch + clen, :] = bterm[c * clen:(c + 1) * clen]
    xpad_ref[0:LRU_PAD, :] = xpad_ref[tm:tm + LRU_PAD, :]

    state = {}
    for g in range(LRU_GROUPS):
        for q in range(chunk_groups):
            state[g, q] = (jnp.zeros((SUBLANES, LANES), F32), jnp.ones((SUBLANES, LANES), F32))
    for s in range(clen):
        if s % (clen // 3 + 1) == 0:
            att_chunks(1)
        for g in range(LRU_GROUPS):
            for q in range(chunk_groups):
                h, p = state[g, q]
                idx = pl.ds(s + q * SUBLANES * pitch, SUBLANES, stride=pitch)
                a_s = a_ref[g, idx, :]
                h = a_s * h + b_ref[g, idx, :]
                p = a_s * p
                h_ref[g, idx, :] = h
                p_ref[g, idx, :] = p
                state[g, q] = (h, p)

    for g in range(LRU_GROUPS):
        if g == 0:
            att_chunks(1)
        cols = slice(g * LANES, (g + 1) * LANES)
        gate_cols = slice(D_LRU + g * LANES, D_LRU + (g + 1) * LANES)
        carry = state_ref[:, cols]
        for c in range(LRU_CHUNKS):
            rows = slice(c * pitch, c * pitch + clen)
            h = h_ref[g, rows, :] + p_ref[g, rows, :] * carry
            out_rows = slice(c * clen, (c + 1) * clen)
            y_ref[out_rows, cols] = (h * _silu(zl_ref[out_rows, gate_cols])).astype(y_ref.dtype)
            h_end, p_end = state[g, c // SUBLANES]
            j = c % SUBLANES
            carry = h_end[j:j + 1, :] + p_end[j:j + 1, :] * carry
        state_ref[:, cols] = carry
    att_chunks(n_att_chunks)


def _in_lru(x, norm_g, scale, shift, w_in_bf16, conv_w, conv_b, wa2, ba, wi2, bi, lam):
    batch, seq, d = x.shape
    n_att = w_in_bf16.shape[1] - 2 * D_LRU
    tm = IN_LRU_ROWS
    scan_rows = LRU_CHUNKS * (tm // LRU_CHUNKS + LRU_PITCH_PAD)

    def whole(shape):
        return pl.BlockSpec(shape, lambda b, i: (0,) * len(shape))

    return pl.pallas_call(
        _in_lru_kernel,
        grid=(batch, seq // tm),
        in_specs=[
            pl.BlockSpec((None, tm, d), lambda b, i: (b, i, 0)),
            whole((1, d)),
            pl.BlockSpec((None, 1, d), lambda b, i: (b, 0, 0)),
            pl.BlockSpec((None, 1, d), lambda b, i: (b, 0, 0)),
            whole(w_in_bf16.shape),
            whole((CONV_W, D_LRU)),
            whole((1, D_LRU)),
            whole((LRU_GROUPS, LANES, LANES)),
            whole((1, D_LRU)),
            whole((LRU_GROUPS, LANES, LANES)),
            whole((1, D_LRU)),
            whole((1, D_LRU)),
        ],
        out_specs=[
            pl.BlockSpec((None, tm, n_att), lambda b, i: (b, i, 0)),
            pl.BlockSpec((None, tm, D_LRU), lambda b, i: (b, i, 0)),
        ],
        out_shape=[
            jax.ShapeDtypeStruct((batch, seq, n_att), F32),
            jax.ShapeDtypeStruct((batch, seq, D_LRU), BF16),
        ],
        scratch_shapes=[
            pltpu.VMEM((tm, 2 * D_LRU), F32),
            pltpu.VMEM((LRU_PAD + tm, D_LRU), F32),
            pltpu.VMEM((LRU_GROUPS, scan_rows, LANES), F32),
            pltpu.VMEM((LRU_GROUPS, scan_rows, LANES), F32),
            pltpu.VMEM((LRU_GROUPS, scan_rows, LANES), F32),
            pltpu.VMEM((LRU_GROUPS, scan_rows, LANES), F32),
            pltpu.VMEM((1, D_LRU), F32),
        ],
        compiler_params=pltpu.CompilerParams(
            dimension_semantics=("arbitrary", "arbitrary"),
            vmem_limit_bytes=VMEM_LIMIT_BYTES),
        name="in_lru",
    )(x, norm_g.reshape(1, d), scale.reshape(batch, 1, d), shift.reshape(batch, 1, d),
      w_in_bf16, conv_w, conv_b.reshape(1, D_LRU), wa2, ba.reshape(1, D_LRU), wi2,
      bi.reshape(1, D_LRU), lam.reshape(1, D_LRU))


HEADS_PER_PAIR = LANES // HEAD_DIM
MOBA_PAIRS = 2
V_AUG_ROWS = HEAD_DIM + BF16_ROWS
SCORE_LOOKAHEAD = 2
SCORE_SLOTS = SCORE_LOOKAHEAD + 1


def _head_rms_scale(x, seg_bf16):
    ms = _dot((x * x).astype(BF16), seg_bf16)
    return x * lax.rsqrt(ms + EPS)


def _moba_kernel(q_ref, k_ref, v_ref, ga_ref, gq_ref, gk_ref, o_ref,
                 qs_ref, kb_ref, va_ref, bias_ref, s_ref, ot_ref):
    seq = q_ref.shape[0]
    npair = q_ref.shape[1] // LANES
    nb = seq // MOBA_BLOCK
    blk = MOBA_BLOCK
    first_ranked = (MOBA_TOPK + 1) * blk

    rr = lax.broadcasted_iota(jnp.int32, (LANES, LANES), 0) // HEAD_DIM
    cc = lax.broadcasted_iota(jnp.int32, (LANES, LANES), 1) // HEAD_DIM
    seg = jnp.where(rr == cc, 1.0 / HEAD_DIM, 0.0).astype(BF16)
    lane_head = lax.broadcasted_iota(jnp.int32, (1, LANES), 1) // HEAD_DIM
    lane_nb = lax.broadcasted_iota(jnp.int32, (nb, LANES), 1) // HEAD_DIM
    blk_row = lax.broadcasted_iota(jnp.int32, (nb, blk), 0)
    gqk = gq_ref[...] * gk_ref[...]

    def setup(c):
        cols = slice(c * LANES, (c + 1) * LANES)
        qn = _head_rms_scale(q_ref[:, cols], seg) * gqk
        kn = _head_rms_scale(k_ref[:, cols], seg)
        kb_ref[c] = kn.astype(BF16)
        vt = v_ref[:, cols].T
        for h in range(HEADS_PER_PAIR):
            q_scale = jnp.where(lane_head == h, HEAD_DIM ** -0.5 * LOG2E, 0.0)
            qs_ref[c, h] = (qn * q_scale).astype(BF16)
            va_ref[c, h, 0:HEAD_DIM, :] = vt[h * HEAD_DIM:(h + 1) * HEAD_DIM].astype(BF16)
            va_ref[c, h, HEAD_DIM:V_AUG_ROWS, :] = jnp.ones((BF16_ROWS, seq), BF16)

        kmean = jnp.concatenate(
            [jnp.mean(kn[n * blk:(n + 1) * blk, :], axis=0, keepdims=True) for n in range(nb)],
            axis=0)
        gmat = jnp.concatenate(
            [jnp.where(lane_nb == h, kmean, 0.0) for h in range(HEADS_PER_PAIR)], axis=0)
        g_hi, g_lo = _split_bf16(gmat)
        q_hi, q_lo = _split_bf16(qn[first_ranked:])
        gate = _dot_nt(g_hi, q_hi) + _dot_nt(g_lo, q_hi) + _dot_nt(g_hi, q_lo)

        for h in range(HEADS_PER_PAIR):
            for qb in range(nb):
                past = blk_row < qb
                if qb > MOBA_TOPK:
                    col = qb * blk - first_ranked
                    g = gate[h * nb:(h + 1) * nb, col:col + blk]
                    cnt = jnp.zeros((nb, blk), jnp.int32)
                    for m in range(qb):
                        gm = g[m:m + 1, :]
                        beats = (gm > g) | ((gm == g) & (m < blk_row))
                        cnt = cnt + beats.astype(jnp.int32)
                    chosen = past & (cnt < MOBA_TOPK)
                else:
                    chosen = past
                bias_ref[c, h, :, qb * blk:(qb + 1) * blk] = jnp.where(chosen, 0.0, MASKED)

    k_pos = lax.broadcasted_iota(jnp.int32, (blk, blk), 0)
    q_pos = lax.broadcasted_iota(jnp.int32, (blk, blk), 1)
    causal = k_pos <= q_pos

    def scores(slot, c, h, qb):
        q_blk = qs_ref[c, h, qb * blk:(qb + 1) * blk, :]
        m_run = None
        biases = []
        s_all = _dot_nt(kb_ref[c, 0:(qb + 1) * blk, :], q_blk)
        for kb in range(qb + 1):
            s = s_all[kb * blk:(kb + 1) * blk, :]
            if kb == qb:
                s = jnp.where(causal, s, MASKED)
            s_ref[slot, kb * blk:(kb + 1) * blk, :] = s
            m_blk = jnp.max(s, axis=0, keepdims=True)
            if kb == qb:
                biases.append(None)
            else:
                b_row = bias_ref[c, h, kb:kb + 1, qb * blk:(qb + 1) * blk]
                m_blk = m_blk + b_row
                biases.append(b_row)
            m_run = m_blk if m_run is None else jnp.maximum(m_run, m_blk)
        return [m_run if b is None else m_run - b for b in biases]

    def softmax(slot, qb, subs):
        ps = []
        for kb in range(qb + 1):
            p = jnp.exp2(s_ref[slot, kb * blk:(kb + 1) * blk, :] - subs[kb])
            ps.append(p.astype(BF16))
        return ps[0] if len(ps) == 1 else jnp.concatenate(ps, axis=0)

    def values(c, h, qb, p_all):
        n_keys = (qb + 1) * blk
        o_aug = _dot(va_ref[c, h, :, 0:n_keys], p_all)
        inv_l = 1.0 / o_aug[HEAD_DIM:HEAD_DIM + 1, :]
        ot_ref[c, h * HEAD_DIM:(h + 1) * HEAD_DIM, qb * blk:(qb + 1) * blk] = (
            o_aug[0:HEAD_DIM, :] * inv_l)

    for c in range(npair):
        setup(c)

    items = [(c, h, qb) for c in range(npair) for h in range(HEADS_PER_PAIR)
             for qb in range(nb)]
    pending = {}
    for i in range(min(SCORE_LOOKAHEAD, len(items))):
        pending[i] = scores(i % SCORE_SLOTS, *items[i])
    probs = None
    for i, item in enumerate(items):
        j = i + SCORE_LOOKAHEAD
        if j < len(items):
            pending[j] = scores(j % SCORE_SLOTS, *items[j])
        new_probs = softmax(i % SCORE_SLOTS, item[2], pending.pop(i))
        if probs is not None:
            values(*items[i - 1], probs)
        probs = new_probs
    values(*items[-1], probs)

    for c in range(npair):
        cols = slice(c * LANES, (c + 1) * LANES)
        o_ref[:, cols] = (ot_ref[c].T * _silu(ga_ref[:, cols])).astype(o_ref.dtype)


def _moba(z, gq2, gk2):
    batch, seq, _ = z.shape
    width = MOBA_PAIRS * LANES
    ncell = D_ATTN // width
    nb = seq // MOBA_BLOCK
    q_off = 0
    k_off = q_off + ncell
    v_off = k_off + ncell
    ga_off = v_off + ncell

    def col_spec(off):
        return pl.BlockSpec((None, seq, width), lambda b, j: (b, 0, off + j))

    gain_spec = pl.BlockSpec((1, LANES), lambda b, j: (0, 0))
    return pl.pallas_call(
        _moba_kernel,
        grid=(batch, ncell),
        in_specs=[col_spec(q_off), col_spec(k_off), col_spec(v_off), col_spec(ga_off),
                  gain_spec, gain_spec],
        out_specs=pl.BlockSpec((None, seq, width), lambda b, j: (b, 0, j)),
        out_shape=jax.ShapeDtypeStruct((batch, seq, D_ATTN), BF16),
        scratch_shapes=[
            pltpu.VMEM((MOBA_PAIRS, HEADS_PER_PAIR, seq, LANES), BF16),
            pltpu.VMEM((MOBA_PAIRS, seq, LANES), BF16),
            pltpu.VMEM((MOBA_PAIRS, HEADS_PER_PAIR, V_AUG_ROWS, seq), BF16),
            pltpu.VMEM((MOBA_PAIRS, HEADS_PER_PAIR, nb, seq), F32),
            pltpu.VMEM((SCORE_SLOTS, seq, MOBA_BLOCK), F32),
            pltpu.VMEM((MOBA_PAIRS, LANES, seq), F32),
        ],
        compiler_params=pltpu.CompilerParams(
            dimension_semantics=("arbitrary", "arbitrary"),
            vmem_limit_bytes=VMEM_LIMIT_BYTES),
        name="moba",
    )(z, z, z, z, gq2, gk2)


def _out_proj_kernel(yl_ref, ya_ref, w_ref, x_ref, gate_ref, o_ref):
    y = _dot(yl_ref[...], w_ref[0:D_LRU, :]) + _dot(ya_ref[...], w_ref[D_LRU:, :])
    o_ref[...] = x_ref[...] + gate_ref[...] * y


def _out_proj(y_lru, y_att, w_out_bf16, x, gate):
    batch, seq, d = x.shape
    tm = OUT_PROJ_ROWS
    return pl.pallas_call(
        _out_proj_kernel,
        grid=(batch, seq // tm),
        in_specs=[
            pl.BlockSpec((None, tm, D_LRU), lambda b, i: (b, i, 0)),
            pl.BlockSpec((None, tm, D_ATTN), lambda b, i: (b, i, 0)),
            pl.BlockSpec((D_LRU + D_ATTN, d), lambda b, i: (0, 0)),
            pl.BlockSpec((None, tm, d), lambda b, i: (b, i, 0)),
            pl.BlockSpec((None, 1, d), lambda b, i: (b, 0, 0)),
        ],
        out_specs=pl.BlockSpec((None, tm, d), lambda b, i: (b, i, 0)),
        out_shape=jax.ShapeDtypeStruct((batch, seq, d), F32),
        compiler_params=pltpu.CompilerParams(
            dimension_semantics=("arbitrary", "arbitrary"),
            vmem_limit_bytes=VMEM_LIMIT_BYTES),
        name="out_proj",
    )(y_lru, y_att, w_out_bf16, x, gate.reshape(batch, 1, d))


def _block_diag_pairs(w):
    nblk, bw, _ = w.shape
    w = w.reshape(nblk // 2, 2, bw, bw)
    zero = jnp.zeros_like(w[:, 0])
    top = jnp.concatenate([w[:, 0], zero], axis=-1)
    bot = jnp.concatenate([zero, w[:, 1]], axis=-1)
    return jnp.concatenate([top, bot], axis=-2)


def _layer(x, c, w_ada, b_ada, norm_g, w_in, conv_w, conv_b, lru_wa, lru_ba, lru_wi,
           lru_bi, lru_lambda, q_norm_g, k_norm_g, w_out):
    d = x.shape[-1]
    ada = _ada(c, w_ada, b_ada)
    shift, scale, gate = ada[:, :d], ada[:, d:2 * d], ada[:, 2 * d:]
    z, y_lru = _in_lru(x, norm_g, scale, shift, w_in.astype(BF16), conv_w, conv_b,
                       _block_diag_pairs(lru_wa).astype(BF16), lru_ba,
                       _block_diag_pairs(lru_wi).astype(BF16), lru_bi, lru_lambda)
    gq2 = jnp.tile(q_norm_g, HEADS_PER_PAIR).reshape(1, LANES)
    gk2 = jnp.tile(k_norm_g, HEADS_PER_PAIR).reshape(1, LANES)
    y_att = _moba(z, gq2, gk2)
    return _out_proj(y_lru, y_att, w_out.astype(BF16), x, gate)


def kernel(x, c, w_ada, b_ada, norm_g, w_in, conv_w, conv_b, lru_wa, lru_ba, lru_wi,
           lru_bi, lru_lambda, q_norm_g, k_norm_g, w_out):
    depth = w_ada.shape[0]
    for l in range(depth):
        x = _layer(x, c, w_ada[l], b_ada[l], norm_g[l], w_in[l], conv_w[l], conv_b[l],
                   lru_wa[l], lru_ba[l], lru_wi[l], lru_bi[l], lru_lambda[l],
                   q_norm_g[l], k_norm_g[l], w_out[l])
    return x
```

```python
import math

import jax
import jax.numpy as jnp
from jax import lax
from jax.experimental import pallas as pl
from jax.experimental.pallas import tpu as pltpu

F32 = jnp.float32
BF16 = jnp.bfloat16

D_MODEL = 1024
D_LRU = 512
D_ATTN = 512
LRU_BLOCK_W = 64
CONV_W = 4
LRU_C = 8.0
HEAD_DIM = 64
MOBA_BLOCK = 256
MOBA_TOPK = 3
EPS = 1e-6
D_IN = 2 * D_LRU + 4 * D_ATTN

SUBLANES = 8
LANES = 128
BF16_ROWS = 16
VMEM_LIMIT_BYTES = 56 * 1024 * 1024
IN_LRU_ROWS = 512
OUT_PROJ_ROWS = 2048

LOG2E = math.log2(math.e)
MASKED = -1e30
TINY = 1e-37


def _dot(a, b):
    return jnp.dot(a, b, preferred_element_type=F32)


def _dot_nt(a, b):
    return lax.dot_general(a, b, (((1,), (1,)), ((), ())), preferred_element_type=F32)


def _split_bf16(x):
    hi = x.astype(BF16)
    lo = (x - hi.astype(F32)).astype(BF16)
    return hi, lo


def _silu(x):
    hx = 0.5 * x
    return hx * (1.0 + jnp.tanh(hx))


ADA_PARTS = 3
ADA_SHIFT, ADA_SCALE, ADA_GATE = range(ADA_PARTS)


def _ada_kernel(c_ref, w_ref, b_ref, win_ref, wout_ref, wa_ref, wi_ref,
                o_ref, win_o_ref, wout_o_ref, wa_o_ref, wi_o_ref):
    sc = _silu(c_ref[...])
    s_hi, s_lo = _split_bf16(sc)
    w_hi, w_lo = _split_bf16(w_ref[...])
    acc = _dot(s_hi, w_hi) + _dot(s_lo, w_hi) + _dot(s_hi, w_lo)
    o_ref[...] = acc + b_ref[...]
    win_o_ref[...] = win_ref[...].astype(BF16)

    @pl.when(pl.program_id(0) == 0)
    def _():
        wout_o_ref[...] = wout_ref[...].astype(BF16)
        for g in range(LRU_GROUPS):
            wa_o_ref[g] = _half_gate_weight(wa_ref, g)
            wi_o_ref[g] = _half_gate_weight(wi_ref, g)


def _ada(c, w_ada, b_ada, w_in, w_out, lru_wa, lru_wi):
    batch, d = c.shape
    assert w_ada.shape == (d, ADA_PARTS * d) and w_in.shape[1] % ADA_PARTS == 0
    bn_in = w_in.shape[1] // ADA_PARTS
    return pl.pallas_call(
        _ada_kernel,
        grid=(ADA_PARTS,),
        in_specs=[
            pl.BlockSpec((batch, d), lambda j: (0, 0)),
            pl.BlockSpec((d, d), lambda j: (0, j)),
            pl.BlockSpec((1, d), lambda j: (0, j)),
            pl.BlockSpec((w_in.shape[0], bn_in), lambda j: (0, j)),
            pl.BlockSpec(w_out.shape, lambda j: (0, 0)),
            pl.BlockSpec(lru_wa.shape, lambda j: (0, 0, 0)),
            pl.BlockSpec(lru_wi.shape, lambda j: (0, 0, 0)),
        ],
        out_specs=[
            pl.BlockSpec((None, batch, d), lambda j: (j, 0, 0)),
            pl.BlockSpec((w_in.shape[0], bn_in), lambda j: (0, j)),
            pl.BlockSpec(w_out.shape, lambda j: (0, 0)),
            pl.BlockSpec((LRU_GROUPS, LANES, LANES), lambda j: (0, 0, 0)),
            pl.BlockSpec((LRU_GROUPS, LANES, LANES), lambda j: (0, 0, 0)),
        ],
        out_shape=[
            jax.ShapeDtypeStruct((ADA_PARTS, batch, d), F32),
            jax.ShapeDtypeStruct(w_in.shape, BF16),
            jax.ShapeDtypeStruct(w_out.shape, BF16),
            jax.ShapeDtypeStruct((LRU_GROUPS, LANES, LANES), BF16),
            jax.ShapeDtypeStruct((LRU_GROUPS, LANES, LANES), BF16),
        ],
        compiler_params=pltpu.CompilerParams(
            dimension_semantics=("arbitrary",), vmem_limit_bytes=VMEM_LIMIT_BYTES),
        name="ada",
    )(c, w_ada, b_ada.reshape(1, ADA_PARTS * d), w_in, w_out, lru_wa, lru_wi)


LRU_PAD = SUBLANES
LRU_CHUNKS = 16
LRU_GROUPS = D_LRU // LANES
LRU_PITCH_PAD = SUBLANES
ATT_CHUNK = 256


def _softplus(x):
    return jnp.maximum(x, 0.0) + jnp.log(1.0 + jnp.exp(-jnp.abs(x)))


def _half_gate_weight(w_ref, g):
    per_group = LANES // LRU_BLOCK_W
    rows = []
    for r in range(per_group):
        blocks = [w_ref[g * per_group + r] if c == r else jnp.zeros((LRU_BLOCK_W, LRU_BLOCK_W), F32)
                  for c in range(per_group)]
        rows.append(jnp.concatenate(blocks, axis=1))
    return (0.5 * jnp.concatenate(rows, axis=0)).astype(BF16)


def _in_lru_kernel(x_ref, g_ref, scale_ref, shift_ref, w_ref, cw_ref, cb_ref, wa_ref, ba_ref,
                   wi_ref, bi_ref, lam_ref, z_ref, y_ref,
                   zl_ref, xpad_ref, a_ref, b_ref, p_ref, h_ref, state_ref):
    tm = x_ref.shape[0]
    clen = tm // LRU_CHUNKS
    pitch = clen + LRU_PITCH_PAD
    chunk_groups = LRU_CHUNKS // SUBLANES
    first_tile = pl.program_id(1) == 0

    x = x_ref[...]
    ms = jnp.mean(x * x, axis=-1, keepdims=True)
    xn = x * lax.rsqrt(ms + EPS) * g_ref[...]
    row = pl.ds(pl.program_id(0), 1)
    hb = (xn * (1.0 + scale_ref[row, :]) + shift_ref[row, :]).astype(BF16)
    zl_ref[...] = _dot(hb, w_ref[:, 0:2 * D_LRU])

    n_att_chunks = z_ref.shape[1] // ATT_CHUNK
    att_chunks_done = [0]

    def att_chunks(count):
        for _ in range(count):
            i = att_chunks_done[0]
            if i < n_att_chunks:
                cols = slice(i * ATT_CHUNK, (i + 1) * ATT_CHUNK)
                z_ref[:, cols] = _dot(
                    hb, w_ref[:, 2 * D_LRU + i * ATT_CHUNK:2 * D_LRU + (i + 1) * ATT_CHUNK])
                att_chunks_done[0] = i + 1

    @pl.when(first_tile)
    def _():
        xpad_ref[0:LRU_PAD, :] = jnp.zeros((LRU_PAD, D_LRU), F32)
        state_ref[...] = jnp.zeros_like(state_ref)

    xpad_ref[LRU_PAD:LRU_PAD + tm, :] = zl_ref[:, 0:D_LRU]
    first_row = lax.broadcasted_iota(jnp.int32, (SUBLANES, LANES), 0) == 0
    starts_sequence = jnp.logical_and(first_row, first_tile)
    for g in range(LRU_GROUPS):
        att_chunks(1)
        cols = slice(g * LANES, (g + 1) * LANES)
        xc = None
        for k in range(CONV_W):
            off = LRU_PAD - (CONV_W - 1) + k
            tap = cw_ref[k:k + 1, cols] * xpad_ref[off:off + tm, cols]
            xc = tap if xc is None else xc + tap
        xc = xc + cb_ref[:, cols]
        xcb = xc.astype(BF16)
        t_r = jnp.tanh(_dot(xcb, wa_ref[g]) + 0.5 * ba_ref[:, cols])
        t_i = jnp.tanh(_dot(xcb, wi_ref[g]) + 0.5 * bi_ref[:, cols])
        half_decay = (-0.5 * LRU_C * LOG2E) * _softplus(-lam_ref[:, cols])
        a = jnp.exp2(t_r * half_decay + half_decay)
        y = 1.0 - a * a
        mult = y * lax.rsqrt(jnp.maximum(y, TINY))
        mult = jnp.concatenate(
            [jnp.where(starts_sequence, 1.0, mult[0:SUBLANES]), mult[SUBLANES:]], axis=0)
        bterm = (mult * xc) * (0.5 * t_i + 0.5)
        for c in range(LRU_CHUNKS):
            a_ref[g, c * pitch:c * pitch + clen, :] = a[c * clen:(c + 1) * clen]
            b_ref[g, c * pitch:c * pitch + clen, :] = bterm[c * clen:(c + 1) * clen]
    xpad_ref[0:LRU_PAD, :] = xpad_ref[tm:tm + LRU_PAD, :]

    state = {}
    for g in range(LRU_GROUPS):
        for q in range(chunk_groups):
            state[g, q] = (jnp.zeros((SUBLANES, LANES), F32), jnp.ones((SUBLANES, LANES), F32))
    for s in range(clen):
        if s % (clen // 3 + 1) == 0:
            att_chunks(1)
        for g in range(LRU_GROUPS):
            for q in range(chunk_groups):
                h, p = state[g, q]
                idx = pl.ds(s + q * SUBLANES * pitch, SUBLANES, stride=pitch)
                a_s = a_ref[g, idx, :]
                h = a_s * h + b_ref[g, idx, :]
                p = a_s * p
                h_ref[g, idx, :] = h
                p_ref[g, idx, :] = p
                state[g, q] = (h, p)

    for g in range(LRU_GROUPS):
        if g == 0:
            att_chunks(1)
        cols = slice(g * LANES, (g + 1) * LANES)
        gate_cols = slice(D_LRU + g * LANES, D_LRU + (g + 1) * LANES)
        carry = state_ref[:, cols]
        for c in range(LRU_CHUNKS):
            rows = slice(c * pitch, c * pitch + clen)
            h = h_ref[g, rows, :] + p_ref[g, rows, :] * carry
            out_rows = slice(c * clen, (c + 1) * clen)
            y_ref[out_rows, cols] = (h * _silu(zl_ref[out_rows, gate_cols])).astype(y_ref.dtype)
            h_end, p_end = state[g, c // SUBLANES]
            j = c % SUBLANES
            carry = h_end[j:j + 1, :] + p_end[j:j + 1, :] * carry
        state_ref[:, cols] = carry
    att_chunks(n_att_chunks)


def _in_lru(x, norm_g, ada, w_in_bf16, conv_w, conv_b, wa, ba, wi, bi, lam):
    batch, seq, d = x.shape
    n_att = w_in_bf16.shape[1] - 2 * D_LRU
    tm = IN_LRU_ROWS
    scan_rows = LRU_CHUNKS * (tm // LRU_CHUNKS + LRU_PITCH_PAD)

    def whole(shape):
        return pl.BlockSpec(shape, lambda b, i: (0,) * len(shape))

    return pl.pallas_call(
        _in_lru_kernel,
        grid=(batch, seq // tm),
        in_specs=[
            pl.BlockSpec((None, tm, d), lambda b, i: (b, i, 0)),
            whole((1, d)),
            pl.BlockSpec((None, batch, d), lambda b, i: (ADA_SCALE, 0, 0)),
            pl.BlockSpec((None, batch, d), lambda b, i: (ADA_SHIFT, 0, 0)),
            whole(w_in_bf16.shape),
            whole((CONV_W, D_LRU)),
            whole((1, D_LRU)),
            whole(wa.shape),
            whole((1, D_LRU)),
            whole(wi.shape),
            whole((1, D_LRU)),
            whole((1, D_LRU)),
        ],
        out_specs=[
            pl.BlockSpec((None, tm, n_att), lambda b, i: (b, i, 0)),
            pl.BlockSpec((None, tm, D_LRU), lambda b, i: (b, i, 0)),
        ],
        out_shape=[
            jax.ShapeDtypeStruct((batch, seq, n_att), F32),
            jax.ShapeDtypeStruct((batch, seq, D_LRU), BF16),
        ],
        scratch_shapes=[
            pltpu.VMEM((tm, 2 * D_LRU), F32),
            pltpu.VMEM((LRU_PAD + tm, D_LRU), F32),
            pltpu.VMEM((LRU_GROUPS, scan_rows, LANES), F32),
            pltpu.VMEM((LRU_GROUPS, scan_rows, LANES), F32),
            pltpu.VMEM((LRU_GROUPS, scan_rows, LANES), F32),
            pltpu.VMEM((LRU_GROUPS, scan_rows, LANES), F32),
            pltpu.VMEM((1, D_LRU), F32),
        ],
        compiler_params=pltpu.CompilerParams(
            dimension_semantics=("arbitrary", "arbitrary"),
            vmem_limit_bytes=VMEM_LIMIT_BYTES),
        name="in_lru",
    )(x, norm_g.reshape(1, d), ada, ada, w_in_bf16, conv_w, conv_b.reshape(1, D_LRU), wa,
      ba.reshape(1, D_LRU), wi, bi.reshape(1, D_LRU), lam.reshape(1, D_LRU))


HEADS_PER_PAIR = LANES // HEAD_DIM
MOBA_PAIRS = 2
V_AUG_ROWS = HEAD_DIM + BF16_ROWS
SCORE_LOOKAHEAD = 2
SCORE_SLOTS = SCORE_LOOKAHEAD + 1


def _head_rms_scale(x, seg_bf16):
    ms = _dot((x * x).astype(BF16), seg_bf16)
    return x * lax.rsqrt(ms + EPS)


def _moba_kernel(q_ref, k_ref, v_ref, ga_ref, gq_ref, gk_ref, o_ref,
                 qs_ref, kb_ref, va_ref, bias_ref, s_ref, ot_ref):
    seq = q_ref.shape[0]
    npair = q_ref.shape[1] // LANES
    nb = seq // MOBA_BLOCK
    blk = MOBA_BLOCK
    first_ranked = (MOBA_TOPK + 1) * blk

    rr = lax.broadcasted_iota(jnp.int32, (LANES, LANES), 0) // HEAD_DIM
    cc = lax.broadcasted_iota(jnp.int32, (LANES, LANES), 1) // HEAD_DIM
    seg = jnp.where(rr == cc, 1.0 / HEAD_DIM, 0.0).astype(BF16)
    lane_head = lax.broadcasted_iota(jnp.int32, (1, LANES), 1) // HEAD_DIM
    lane_nb = lax.broadcasted_iota(jnp.int32, (nb, LANES), 1) // HEAD_DIM
    blk_row = lax.broadcasted_iota(jnp.int32, (nb, blk), 0)
    gqk = jnp.concatenate([gq_ref[...] * gk_ref[...]] * HEADS_PER_PAIR, axis=1)

    def setup(c):
        cols = slice(c * LANES, (c + 1) * LANES)
        qn = _head_rms_scale(q_ref[:, cols], seg) * gqk
        kn = _head_rms_scale(k_ref[:, cols], seg)
        kb_ref[c] = kn.astype(BF16)
        vt = v_ref[:, cols].T
        for h in range(HEADS_PER_PAIR):
            q_scale = jnp.where(lane_head == h, HEAD_DIM ** -0.5 * LOG2E, 0.0)
            qs_ref[c, h] = (qn * q_scale).astype(BF16)
            va_ref[c, h, 0:HEAD_DIM, :] = vt[h * HEAD_DIM:(h + 1) * HEAD_DIM].astype(BF16)
            va_ref[c, h, HEAD_DIM:V_AUG_ROWS, :] = jnp.ones((BF16_ROWS, seq), BF16)

        kmean = jnp.concatenate(
            [jnp.mean(kn[n * blk:(n + 1) * blk, :], axis=0, keepdims=True) for n in range(nb)],
            axis=0)
        gmat = jnp.concatenate(
            [jnp.where(lane_nb == h, kmean, 0.0) for h in range(HEADS_PER_PAIR)], axis=0)
        g_hi, g_lo = _split_bf16(gmat)
        q_hi, q_lo = _split_bf16(qn[first_ranked:])
        gate = _dot_nt(g_hi, q_hi) + _dot_nt(g_lo, q_hi) + _dot_nt(g_hi, q_lo)

        for h in range(HEADS_PER_PAIR):
            for qb in range(nb):
                past = blk_row < qb
                if qb > MOBA_TOPK:
                    col = qb * blk - first_ranked
                    g = gate[h * nb:(h + 1) * nb, col:col + blk]
                    cnt = jnp.zeros((nb, blk), jnp.int32)
                    for m in range(qb):
                        gm = g[m:m + 1, :]
                        beats = (gm > g) | ((gm == g) & (m < blk_row))
                        cnt = cnt + beats.astype(jnp.int32)
                    chosen = past & (cnt < MOBA_TOPK)
                else:
                    chosen = past
                bias_ref[c, h, :, qb * blk:(qb + 1) * blk] = jnp.where(chosen, 0.0, MASKED)

    k_pos = lax.broadcasted_iota(jnp.int32, (blk, blk), 0)
    q_pos = lax.broadcasted_iota(jnp.int32, (blk, blk), 1)
    causal = k_pos <= q_pos

    def scores(slot, c, h, qb):
        q_blk = qs_ref[c, h, qb * blk:(qb + 1) * blk, :]
        m_run = None
        biases = []
        s_all = _dot_nt(kb_ref[c, 0:(qb + 1) * blk, :], q_blk)
        for kb in range(qb + 1):
            s = s_all[kb * blk:(kb + 1) * blk, :]
            if kb == qb:
                s = jnp.where(causal, s, MASKED)
            s_ref[slot, kb * blk:(kb + 1) * blk, :] = s
            m_blk = jnp.max(s, axis=0, keepdims=True)
            if kb == qb:
                biases.append(None)
            else:
                b_row = bias_ref[c, h, kb:kb + 1, qb * blk:(qb + 1) * blk]
                m_blk = m_blk + b_row
                biases.append(b_row)
            m_run = m_blk if m_run is None else jnp.maximum(m_run, m_blk)
        return [m_run if b is None else m_run - b for b in biases]

    def softmax(slot, qb, subs):
        ps = []
        for kb in range(qb + 1):
            p = jnp.exp2(s_ref[slot, kb * blk:(kb + 1) * blk, :] - subs[kb])
            ps.append(p.astype(BF16))
        return ps[0] if len(ps) == 1 else jnp.concatenate(ps, axis=0)

    def values(c, h, qb, p_all):
        n_keys = (qb + 1) * blk
        o_aug = _dot(va_ref[c, h, :, 0:n_keys], p_all)
        inv_l = 1.0 / o_aug[HEAD_DIM:HEAD_DIM + 1, :]
        ot_ref[c, h * HEAD_DIM:(h + 1) * HEAD_DIM, qb * blk:(qb + 1) * blk] = (
            o_aug[0:HEAD_DIM, :] * inv_l)

    for c in range(npair):
        setup(c)

    items = [(c, h, qb) for c in range(npair) for h in range(HEADS_PER_PAIR)
             for qb in range(nb)]
    pending = {}
    for i in range(min(SCORE_LOOKAHEAD, len(items))):
        pending[i] = scores(i % SCORE_SLOTS, *items[i])
    probs = None
    for i, item in enumerate(items):
        j = i + SCORE_LOOKAHEAD
        if j < len(items):
            pending[j] = scores(j % SCORE_SLOTS, *items[j])
        new_probs = softmax(i % SCORE_SLOTS, item[2], pending.pop(i))
        if probs is not None:
            values(*items[i - 1], probs)
        probs = new_probs
    values(*items[-1], probs)

    for c in range(npair):
        cols = slice(c * LANES, (c + 1) * LANES)
        o_ref[:, cols] = (ot_ref[c].T * _silu(ga_ref[:, cols])).astype(o_ref.dtype)


def _moba(z, q_gain, k_gain):
    batch, seq, _ = z.shape
    width = MOBA_PAIRS * LANES
    ncell = D_ATTN // width
    nb = seq // MOBA_BLOCK
    q_off = 0
    k_off = q_off + ncell
    v_off = k_off + ncell
    ga_off = v_off + ncell

    def col_spec(off):
        return pl.BlockSpec((None, seq, width), lambda b, j: (b, 0, off + j))

    gain_spec = pl.BlockSpec((1, HEAD_DIM), lambda b, j: (0, 0))
    return pl.pallas_call(
        _moba_kernel,
        grid=(batch, ncell),
        in_specs=[col_spec(q_off), col_spec(k_off), col_spec(v_off), col_spec(ga_off),
                  gain_spec, gain_spec],
        out_specs=pl.BlockSpec((None, seq, width), lambda b, j: (b, 0, j)),
        out_shape=jax.ShapeDtypeStruct((batch, seq, D_ATTN), BF16),
        scratch_shapes=[
            pltpu.VMEM((MOBA_PAIRS, HEADS_PER_PAIR, seq, LANES), BF16),
            pltpu.VMEM((MOBA_PAIRS, seq, LANES), BF16),
            pltpu.VMEM((MOBA_PAIRS, HEADS_PER_PAIR, V_AUG_ROWS, seq), BF16),
            pltpu.VMEM((MOBA_PAIRS, HEADS_PER_PAIR, nb, seq), F32),
            pltpu.VMEM((SCORE_SLOTS, seq, MOBA_BLOCK), F32),
            pltpu.VMEM((MOBA_PAIRS, LANES, seq), F32),
        ],
        compiler_params=pltpu.CompilerParams(
            dimension_semantics=("arbitrary", "arbitrary"),
            vmem_limit_bytes=VMEM_LIMIT_BYTES),
        name="moba",
    )(z, z, z, z, q_gain.reshape(1, HEAD_DIM), k_gain.reshape(1, HEAD_DIM))


def _out_proj_kernel(yl_ref, ya_ref, w_ref, x_ref, gate_ref, o_ref):
    y = _dot(yl_ref[...], w_ref[0:D_LRU, :]) + _dot(ya_ref[...], w_ref[D_LRU:, :])
    o_ref[...] = x_ref[...] + gate_ref[pl.ds(pl.program_id(0), 1), :] * y


def _out_proj(y_lru, y_att, w_out_bf16, x, ada):
    batch, seq, d = x.shape
    tm = OUT_PROJ_ROWS
    return pl.pallas_call(
        _out_proj_kernel,
        grid=(batch, seq // tm),
        in_specs=[
            pl.BlockSpec((None, tm, D_LRU), lambda b, i: (b, i, 0)),
            pl.BlockSpec((None, tm, D_ATTN), lambda b, i: (b, i, 0)),
            pl.BlockSpec((D_LRU + D_ATTN, d), lambda b, i: (0, 0)),
            pl.BlockSpec((None, tm, d), lambda b, i: (b, i, 0)),
            pl.BlockSpec((None, batch, d), lambda b, i: (ADA_GATE, 0, 0)),
        ],
        out_specs=pl.BlockSpec((None, tm, d), lambda b, i: (b, i, 0)),
        out_shape=jax.ShapeDtypeStruct((batch, seq, d), F32),
        compiler_params=pltpu.CompilerParams(
            dimension_semantics=("arbitrary", "arbitrary"),
            vmem_limit_bytes=VMEM_LIMIT_BYTES),
        name="out_proj",
    )(y_lru, y_att, w_out_bf16, x, ada)


def _layer(x, c, w_ada, b_ada, norm_g, w_in, conv_w, conv_b, lru_wa, lru_ba, lru_wi,
           lru_bi, lru_lambda, q_norm_g, k_norm_g, w_out):
    ada, w_in_bf16, w_out_bf16, wa_half, wi_half = _ada(c, w_ada, b_ada, w_in, w_out, lru_wa,
                                                        lru_wi)
    z, y_lru = _in_lru(x, norm_g, ada, w_in_bf16, conv_w, conv_b, wa_half, lru_ba, wi_half,
                       lru_bi, lru_lambda)
    y_att = _moba(z, q_norm_g, k_norm_g)
    return _out_proj(y_lru, y_att, w_out_bf16, x, ada)


def kernel(x, c, w_ada, b_ada, norm_g, w_in, conv_w, conv_b, lru_wa, lru_ba, lru_wi,
           lru_bi, lru_lambda, q_norm_g, k_norm_g, w_out):
    depth = w_ada.shape[0]
    for l in range(depth):
        x = _layer(x, c, w_ada[l], b_ada[l], norm_g[l], w_in[l], conv_w[l], conv_b[l],
                   lru_wa[l], lru_ba[l], lru_wi[l], lru_bi[l], lru_lambda[l],
                   q_norm_g[l], k_norm_g[l], w_out[l])
    return x
```
